```python
import math
import jax
import jax.numpy as jnp
from jax import lax
import numpy as np

D_MODEL = 1024
BATCH = 8
SEQ = 2048
DEPTH = 1
DEC_BATCH = 128
DEC_SEQ = 1
PAST_LEN = 8192
PAGE_SIZE = 128

A_HEADS = 8
A_HD = 64
A_QK = A_HEADS * 2 * A_HD
A_V = A_HEADS * 2 * A_HD
A_SCALE = A_HD ** -0.5
Q_BLOCK = 128
B_HEADS = 8
B_DK = 128
B_DV = D_MODEL // B_HEADS
B_K = B_HEADS * B_DK
B_V = B_HEADS * B_DV
CHUNK = 64
N_BUCKETS = 32
MAX_DISTANCE = 128
N_EXPERTS = 32
TOP_K = 4
D_FF = D_MODEL
SWIGLU_ALPHA = 1.702
SWIGLU_LIMIT = 7.0
EPS = 1e-5
NEG = -1e30

IN_WIDTHS = (A_QK, A_QK, A_V, B_K, B_K, B_V, B_V, D_MODEL, D_MODEL)
D_IN = sum(IN_WIDTHS)
IN_SPLIT_IDX = tuple(int(i) for i in np.cumsum(IN_WIDTHS)[:-1])

kernel_name = "hybrid_diffattn_hgrn2_moe_step"


def rmsnorm(x, w):
    x32 = x.astype(jnp.float32)
    y = x32 * lax.rsqrt(jnp.mean(x32 * x32, axis=-1, keepdims=True) + EPS)
    return y.astype(x.dtype) * w


def t5_bucket(rel):
    n = jnp.maximum(rel, 0)
    max_exact = N_BUCKETS // 2
    nf = jnp.maximum(n, max_exact).astype(jnp.float32)
    large = max_exact + (jnp.log(nf / max_exact) / math.log(MAX_DISTANCE / max_exact)
                         * (N_BUCKETS - max_exact)).astype(jnp.int32)
    return jnp.where(n < max_exact, n, jnp.minimum(large, N_BUCKETS - 1))


def diff_lambda(lq1, lk1, lq2, lk2, lam_init):
    e1 = jnp.exp(jnp.sum(lq1.astype(jnp.float32) * lk1.astype(jnp.float32)))
    e2 = jnp.exp(jnp.sum(lq2.astype(jnp.float32) * lk2.astype(jnp.float32)))
    return e1 - e2 + lam_init


def diff_attn_core(q, k, v, bias, mask, lam):
    s = jnp.einsum("...qhcd,...khcd->...hcqk", q, k).astype(jnp.float32) * A_SCALE
    s = s + jnp.moveaxis(bias, -1, 0)[:, None].astype(jnp.float32)
    s = jnp.where(mask, s, NEG)
    p = jax.nn.softmax(s, axis=-1)
    w = p[..., 0, :, :] - lam * p[..., 1, :, :]
    return jnp.einsum("...hqk,...khe->...qhe", w.astype(v.dtype), v)


def diff_attn_prompt(qa, ka, va, rel_bias, lam):
    b, s, _ = qa.shape
    q = qa.reshape(b, s, A_HEADS, 2, A_HD)
    k = ka.reshape(b, s, A_HEADS, 2, A_HD)
    v = va.reshape(b, s, A_HEADS, 2 * A_HD)
    nb = s // Q_BLOCK
    qb = q.reshape(b, nb, Q_BLOCK, A_HEADS, 2, A_HD).swapaxes(0, 1)
    kpos = jnp.arange(s)

    def block(args):
        q_i, start = args
        rel = (start + jnp.arange(Q_BLOCK))[:, None] - kpos[None, :]
        return diff_attn_core(q_i, k, v, rel_bias[t5_bucket(rel)], rel >= 0, lam)

    o = lax.map(block, (qb, jnp.arange(nb) * Q_BLOCK))
    o = o.swapaxes(0, 1).reshape(b, s, A_HEADS, 2 * A_HD)
    return o, k.reshape(b, s, A_HEADS, 2 * A_HD), v


def diff_attn_sample(qa, ka, va, cache_k, cache_v, layer, page_table, rel_bias, lam):
    n_seq, t_new, _ = qa.shape
    past = page_table.shape[1] * PAGE_SIZE
    q = qa.reshape(n_seq, t_new, A_HEADS, 2, A_HD)
    k_new = ka.reshape(n_seq, t_new, A_HEADS, 2 * A_HD)
    v_new = va.reshape(n_seq, t_new, A_HEADS, 2 * A_HD)
    rel = (past + jnp.arange(t_new))[:, None] - jnp.arange(past + t_new)[None, :]
    bias = rel_bias[t5_bucket(rel)]
    mask = rel >= 0

    def one_seq(args):
        q_i, k_i, v_i, pages = args
        k_all = jnp.concatenate([cache_k[layer, pages].reshape(past, A_HEADS, 2 * A_HD), k_i], axis=0)
        v_all = jnp.concatenate([cache_v[layer, pages].reshape(past, A_HEADS, 2 * A_HD), v_i], axis=0)
        return diff_attn_core(q_i, k_all.reshape(past + t_new, A_HEADS, 2, A_HD), v_all, bias, mask, lam)

    o = lax.map(one_seq, (q, k_new, v_new, page_table))
    return o, k_new, v_new


def hgrn_inputs(qb, fb, ib, lb):
    shp_k = qb.shape[:-1] + (B_HEADS, B_DK)
    lb = lb.reshape(B_HEADS, B_DK)
    q = jax.nn.silu(qb.astype(jnp.float32)).reshape(shp_k)
    f = lb + (1.0 - lb) * jax.nn.sigmoid(fb.astype(jnp.float32)).reshape(shp_k)
    v = ib.astype(jnp.float32).reshape(qb.shape[:-1] + (B_HEADS, B_DV))
    return q, 1.0 - f, v, jnp.log(f)


def hgrn_chunked(q, k, v, logf, s0):
    b, t, h, _ = q.shape
    nc = t // CHUNK

    def to_chunks(a):
        return a.reshape(b, nc, CHUNK, h, a.shape[-1]).transpose(1, 0, 3, 2, 4)

    causal = jnp.tril(jnp.ones((CHUNK, CHUNK), dtype=bool))[:, :, None]

    def body(state, inp):
        qc, kc, vc, gc = inp
        cum = jnp.cumsum(gc, axis=2)
        o_inter = jnp.einsum("bhtk,bhkv->bhtv", qc * jnp.exp(cum), state)
        diff = cum[:, :, :, None, :] - cum[:, :, None, :, :]
        decay = jnp.exp(jnp.where(causal, diff, -jnp.inf))
        att = jnp.einsum("bhtk,bhsk,bhtsk->bhts", qc, kc, decay)
        o = o_inter + jnp.einsum("bhts,bhsv->bhtv", att, vc)
        last = cum[:, :, -1:, :]
        state = (jnp.exp(last[:, :, 0, :])[..., None] * state
                 + jnp.einsum("bhsk,bhsv->bhkv", kc * jnp.exp(last - cum), vc))
        return state, o

    state, o = lax.scan(body, s0, (to_chunks(q), to_chunks(k), to_chunks(v), to_chunks(logf)))
    return o.transpose(1, 0, 3, 2, 4).reshape(b, t, h, v.shape[-1]), state


def hgrn_step(q, k, v, logf, s0):
    def step(state, inp):
        qt, kt, vt, gt = inp
        state = jnp.exp(gt)[..., None] * state + kt[..., None] * vt[..., None, :]
        return state, jnp.einsum("bhk,bhkv->bhv", qt, state)

    state, o = lax.scan(step, s0.astype(jnp.float32),
                        (q.swapaxes(0, 1), k.swapaxes(0, 1), v.swapaxes(0, 1), logf.swapaxes(0, 1)))
    return o.swapaxes(0, 1), state


def clamped_swiglu(u):
    glu, lin = u[..., ::2], u[..., 1::2]
    glu = jnp.minimum(glu, SWIGLU_LIMIT)
    lin = jnp.clip(lin, -SWIGLU_LIMIT, SWIGLU_LIMIT)
    return glu * jax.nn.sigmoid(SWIGLU_ALPHA * glu) * (lin + 1.0)


def moe(h, w_router, b_router, w1, b1, w2, b2):
    shp = h.shape
    t = h.reshape(-1, shp[-1])
    logits = (t @ w_router).astype(jnp.float32) + b_router.astype(jnp.float32)
    top_v, top_i = lax.top_k(logits, TOP_K)
    gates = jax.nn.softmax(top_v, axis=-1)
    flat_e = top_i.reshape(-1)
    order = jnp.argsort(flat_e)
    e_sorted = flat_e[order]
    tok = order // TOP_K
    sizes = jnp.bincount(flat_e, length=N_EXPERTS).astype(jnp.int32)
    u = lax.ragged_dot(t[tok], w1, sizes) + b1[e_sorted]
    y = lax.ragged_dot(clamped_swiglu(u), w2, sizes) + b2[e_sorted]
    y = y * gates.reshape(-1)[order][:, None].astype(y.dtype)
    return jnp.zeros_like(t).at[tok].add(y).reshape(shp)


def pre_mix(x, norm_w, w_in):
    return jnp.split(rmsnorm(x, norm_w) @ w_in, IN_SPLIT_IDX, axis=-1)


def post_mix(x, oa, ob, og, ga, gb, subln_w, lam_init, hgrn_norm_w, w_pa, w_pb, w_o,
             norm2_w, w_router, b_router, w1, b1, w2, b2):
    oa = (rmsnorm(oa, subln_w) * (1.0 - lam_init)).reshape(oa.shape[:-2] + (A_V,))
    ob = ob * jax.nn.sigmoid(og.astype(jnp.float32)).reshape(ob.shape)
    ob = rmsnorm(ob, hgrn_norm_w).astype(x.dtype).reshape(ob.shape[:-2] + (B_V,))
    m = jax.nn.sigmoid(ga) * (oa @ w_pa) + jax.nn.sigmoid(gb) * (ob @ w_pb)
    x = x + m @ w_o
    return x + moe(rmsnorm(x, norm2_w), w_router, b_router, w1, b1, w2, b2)


def setup_inputs(seed: int = 0) -> dict:
    key = jax.random.key(seed)
    ks = jax.random.split(key, 32)
    f32 = jnp.float32
    n_pages = PAST_LEN // PAGE_SIZE
    n_phys = (DEC_BATCH * n_pages * 5) // 4

    def nrm(k, shape, scale):
        return jax.random.normal(k, shape, f32) * scale

    def gain(k, shape):
        return 1.0 + nrm(k, shape, 0.02)

    page_table = jax.random.permutation(ks[5], n_phys)[:DEC_BATCH * n_pages]
    page_table = page_table.reshape(DEC_BATCH, n_pages).astype(jnp.int32)
    return {
        "x_prompt": nrm(ks[0], (BATCH, SEQ, D_MODEL), 1.0),
        "x_sample": nrm(ks[1], (DEC_BATCH, DEC_SEQ, D_MODEL), 1.0),
        "cache_k": nrm(ks[2], (DEPTH, n_phys, PAGE_SIZE, A_HEADS, 2 * A_HD), 1.0),
        "cache_v": nrm(ks[3], (DEPTH, n_phys, PAGE_SIZE, A_HEADS, 2 * A_HD), 1.0),
        "state_hgrn": nrm(ks[4], (DEPTH, DEC_BATCH, B_HEADS, B_DK, B_DV), 0.5),
        "page_table": page_table,
        "norm1_w": gain(ks[6], (DEPTH, D_MODEL)),
        "w_in": nrm(ks[7], (DEPTH, D_MODEL, D_IN), D_MODEL ** -0.5),
        "lambda_q1": nrm(ks[8], (DEPTH, A_HD), 0.1),
        "lambda_k1": nrm(ks[9], (DEPTH, A_HD), 0.1),
        "lambda_q2": nrm(ks[10], (DEPTH, A_HD), 0.1),
        "lambda_k2": nrm(ks[11], (DEPTH, A_HD), 0.1),
        "subln_w": gain(ks[12], (DEPTH, 2 * A_HD)),
        "lb_logits": nrm(ks[13], (DEPTH + 1, B_K), 0.1),
        "hgrn_norm_w": gain(ks[14], (DEPTH, B_DV)),
        "rel_bias": nrm(ks[15], (N_BUCKETS, A_HEADS), 0.5),
        "w_pa": nrm(ks[16], (DEPTH, A_V, D_MODEL), A_V ** -0.5),
        "w_pb": nrm(ks[17], (DEPTH, B_V, D_MODEL), B_V ** -0.5),
        "w_o": nrm(ks[18], (DEPTH, D_MODEL, D_MODEL), D_MODEL ** -0.5),
        "norm2_w": gain(ks[19], (DEPTH, D_MODEL)),
        "w_router": nrm(ks[20], (DEPTH, D_MODEL, N_EXPERTS), D_MODEL ** -0.5),
        "b_router": nrm(ks[21], (DEPTH, N_EXPERTS), 0.01),
        "w1": nrm(ks[22], (DEPTH, N_EXPERTS, D_MODEL, 2 * D_FF), D_MODEL ** -0.5),
        "b1": nrm(ks[23], (DEPTH, N_EXPERTS, 2 * D_FF), 0.02),
        "w2": nrm(ks[24], (DEPTH, N_EXPERTS, D_FF, D_MODEL), D_FF ** -0.5),
        "b2": nrm(ks[25], (DEPTH, N_EXPERTS, D_MODEL), 0.02),
        "final_norm_w": gain(ks[26], (D_MODEL,)),
    }


def reference(x_prompt, x_sample, cache_k, cache_v, state_hgrn, page_table, norm1_w, w_in,
              lambda_q1, lambda_k1, lambda_q2, lambda_k2, subln_w, lb_logits, hgrn_norm_w, rel_bias,
              w_pa, w_pb, w_o, norm2_w, w_router, b_router, w1, b1, w2, b2, final_norm_w):
    lower_bounds = jnp.cumsum(jax.nn.softmax(lb_logits.astype(jnp.float32), axis=0), axis=0)
    xp, xs = x_prompt, x_sample
    kp_l, vp_l, sp_l, ks_l, vs_l, ss_l = [], [], [], [], [], []
    for l in range(DEPTH):
        lam_init = 0.8 - 0.6 * math.exp(-0.3 * l)
        lam = diff_lambda(lambda_q1[l], lambda_k1[l], lambda_q2[l], lambda_k2[l], lam_init)
        lb = lower_bounds[l]
        layer_w = (subln_w[l], lam_init, hgrn_norm_w[l], w_pa[l], w_pb[l], w_o[l], norm2_w[l],
                   w_router[l], b_router[l], w1[l], b1[l], w2[l], b2[l])

        qa, ka, va, qb, fb, ib, og, ga, gb = pre_mix(xp, norm1_w[l], w_in[l])
        oa, k_p, v_p = diff_attn_prompt(qa, ka, va, rel_bias, lam)
        q, k, v, g = hgrn_inputs(qb, fb, ib, lb)
        s0 = jnp.zeros((xp.shape[0], B_HEADS, B_DK, B_DV), jnp.float32)
        ob, s_p = hgrn_chunked(q, k, v, g, s0)
        xp = post_mix(xp, oa, ob, og, ga, gb, *layer_w)
        kp_l.append(k_p)
        vp_l.append(v_p)
        sp_l.append(s_p.astype(state_hgrn.dtype))

        qa, ka, va, qb, fb, ib, og, ga, gb = pre_mix(xs, norm1_w[l], w_in[l])
        oa, k_s, v_s = diff_attn_sample(qa, ka, va, cache_k, cache_v, l, page_table, rel_bias, lam)
        q, k, v, g = hgrn_inputs(qb, fb, ib, lb)
        ob, s_s = hgrn_step(q, k, v, g, state_hgrn[l])
        xs = post_mix(xs, oa, ob, og, ga, gb, *layer_w)
        ks_l.append(k_s)
        vs_l.append(v_s)
        ss_l.append(s_s.astype(state_hgrn.dtype))

    return (rmsnorm(xp, final_norm_w), rmsnorm(xs, final_norm_w), jnp.stack(kp_l), jnp.stack(vp_l),
            jnp.stack(sp_l), jnp.stack(ks_l), jnp.stack(vs_l), jnp.stack(ss_l))
```

```python
import functools
import math

import numpy as np
import jax
import jax.numpy as jnp
from jax import lax
from jax.experimental import pallas as pl
from jax.experimental.pallas import tpu as pltpu

F32 = jnp.float32
BF16 = jnp.bfloat16
I32 = jnp.int32

D_MODEL = 1024
N_HEADS = 8
HEAD_W = 128
A_HD = 64
A_SCALE = A_HD ** -0.5
PAGE = 128
N_BUCKETS = 32
MAX_DISTANCE = 128
N_EXPERTS = 32
TOP_K = 4
SWIGLU_ALPHA = 1.702
SWIGLU_LIMIT = 7.0
EPS = 1e-5
NEG = -1e30

LANES = 128
VMEM_LIMIT = 56 * 1024 * 1024

TOK_TILE = 128
ROW_TILE = 256
HGRN_CHUNK = 64
HGRN_SUB = 16


def _cparams(*sem):
    return pltpu.CompilerParams(dimension_semantics=sem, vmem_limit_bytes=VMEM_LIMIT)


def _sigmoid(x):
    return 1.0 / (1.0 + jnp.exp(-x))


def _dot(a, b):
    return jnp.dot(a, b, preferred_element_type=F32)


def _dot_nt(a, b):
    return lax.dot_general(a, b, (((1,), (1,)), ((), ())), preferred_element_type=F32)


def _dot_tn(a, b):
    return lax.dot_general(a, b, (((0,), (0,)), ((), ())), preferred_element_type=F32)


def _lam(lam_ref, lam_init):
    lv = lam_ref[...]
    e1 = jnp.exp(jnp.sum(lv[0:1] * lv[1:2], axis=-1, keepdims=True))
    e2 = jnp.exp(jnp.sum(lv[2:3] * lv[3:4], axis=-1, keepdims=True))
    return e1 - e2 + lam_init


def _head_rmsnorm(o, w):
    return o * lax.rsqrt(jnp.mean(o * o, axis=-1, keepdims=True) + EPS) * w


def _inproj_kernel(x_ref, nw_ref, lb_ref, w_ref, qa_ref, ka_ref, va_ref, hq_ref, hk_ref, hg_ref,
                   hv_ref, sog_ref, sga_ref, sgb_ref):
    x = x_ref[...]
    h = x * lax.rsqrt(jnp.mean(x * x, axis=-1, keepdims=True) + EPS) * nw_ref[...]
    hb = h.astype(BF16)

    def proj(s):
        return _dot(hb, w_ref[:, s * D_MODEL:(s + 1) * D_MODEL])

    qa_ref[...] = (proj(0) * A_SCALE).astype(BF16)
    ka_ref[...] = proj(1)
    va_ref[...] = proj(2)
    qb = proj(3)
    hq_ref[...] = qb * _sigmoid(qb)
    lb = lb_ref[...]
    f = lb + (1.0 - lb) * _sigmoid(proj(4))
    hk_ref[...] = 1.0 - f
    hg_ref[...] = jnp.log(f)
    hv_ref[...] = proj(5).astype(BF16)
    sog_ref[...] = _sigmoid(proj(6))
    sga_ref[...] = _sigmoid(proj(7))
    sgb_ref[...] = _sigmoid(proj(8))


def _inproj(x, norm_w, lb, w_in_bf, tm):
    n = x.shape[0]
    row = lambda i: (i, 0)
    fixed = lambda i: (0, 0)
    f32_out = jax.ShapeDtypeStruct((n, D_MODEL), F32)
    bf_out = jax.ShapeDtypeStruct((n, D_MODEL), BF16)
    blk = pl.BlockSpec((tm, D_MODEL), row)
    return pl.pallas_call(
        _inproj_kernel,
        grid=(n // tm,),
        in_specs=[blk,
                  pl.BlockSpec((1, D_MODEL), fixed),
                  pl.BlockSpec((1, D_MODEL), fixed),
                  pl.BlockSpec(w_in_bf.shape, fixed, pipeline_mode=pl.Buffered(1))],
        out_specs=[blk] * 10,
        out_shape=[bf_out, f32_out, f32_out, f32_out, f32_out, f32_out, bf_out, f32_out, f32_out, f32_out],
        compiler_params=_cparams("parallel"),
        name="inproj",
    )(x, norm_w, lb, w_in_bf)


def _attn_kernel(lam_ref, sw_ref, q_ref, k_ref, v_ref, bias_ref, o_ref, *, tile, lam_init):
    qi = pl.program_id(2)
    q = q_ref[...]
    lane = lax.broadcasted_iota(I32, q.shape, 1)
    q1 = jnp.where(lane < A_HD, q, jnp.zeros_like(q))
    q2 = jnp.where(lane >= A_HD, q, jnp.zeros_like(q))

    def body(kj, carry):
        m1, l1, a1, m2, l2, a2 = carry
        sl = pl.ds(pl.multiple_of(kj * tile, tile), tile)
        kb = k_ref[sl, :].astype(BF16)
        vb = v_ref[sl, :].astype(BF16)
        bias = bias_ref[jnp.minimum(qi - kj, 2)]

        def upd(qz, m, l, a):
            s = _dot_nt(qz, kb) + bias
            mn = jnp.maximum(m, jnp.max(s, axis=-1, keepdims=True))
            p = jnp.exp(s - mn)
            al = jnp.exp(m - mn)
            return mn, al * l + jnp.sum(p, axis=-1, keepdims=True), al * a + _dot(p.astype(BF16), vb)

        m1, l1, a1 = upd(q1, m1, l1, a1)
        m2, l2, a2 = upd(q2, m2, l2, a2)
        return m1, l1, a1, m2, l2, a2

    mi = jnp.full((tile, 1), NEG, F32)
    li = jnp.zeros((tile, 1), F32)
    ai = jnp.zeros((tile, HEAD_W), F32)
    m1, l1, a1, m2, l2, a2 = lax.fori_loop(0, qi + 1, body, (mi, li, ai, mi, li, ai))
    lam = _lam(lam_ref, lam_init)
    o = a1 / l1 - lam * (a2 / l2)
    o_ref[...] = (_head_rmsnorm(o, sw_ref[...]) * (1.0 - lam_init)).astype(BF16)


def _attn_prompt(lamv, subln_w, qa, ka, va, bias_tab, b, s, tile, lam_init):
    nq = s // tile
    return pl.pallas_call(
        functools.partial(_attn_kernel, tile=tile, lam_init=lam_init),
        grid=(b, N_HEADS, nq),
        in_specs=[pl.BlockSpec((4, A_HD), lambda bi, h, qi: (0, 0)),
                  pl.BlockSpec((1, HEAD_W), lambda bi, h, qi: (0, 0)),
                  pl.BlockSpec((tile, HEAD_W), lambda bi, h, qi: (bi * nq + qi, h)),
                  pl.BlockSpec((s, HEAD_W), lambda bi, h, qi: (bi, h)),
                  pl.BlockSpec((s, HEAD_W), lambda bi, h, qi: (bi, h)),
                  pl.BlockSpec((None, 3, tile, tile), lambda bi, h, qi: (h, 0, 0, 0))],
        out_specs=pl.BlockSpec((tile, HEAD_W), lambda bi, h, qi: (bi * nq + qi, h)),
        out_shape=jax.ShapeDtypeStruct((b * s, D_MODEL), BF16),
        compiler_params=_cparams("parallel", "parallel", "parallel"),
        name="attn_prompt",
    )(lamv, subln_w, qa, ka, va, bias_tab)


def _split3(g):
    g1 = g.astype(BF16)
    r1 = g - g1.astype(F32)
    g2 = r1.astype(BF16)
    g3 = (r1 - g2.astype(F32)).astype(BF16)
    return g1, g2, g3


def _hgrn_kernel(q_ref, k_ref, g_ref, v_ref, og_ref, nw_ref, o_ref, s_ref, st_ref, *, seq):
    c, sub = HGRN_CHUNK, HGRN_SUB
    st_ref[...] = jnp.zeros_like(st_ref)
    row = lax.broadcasted_iota(I32, (c, c), 0)
    col = lax.broadcasted_iota(I32, (c, c), 1)
    tril = jnp.where(row >= col, 1.0, 0.0).astype(BF16)
    rowk = lax.broadcasted_iota(I32, (c, HEAD_W), 0)
    srow = lax.broadcasted_iota(I32, (sub, c), 0)
    scol = lax.broadcasted_iota(I32, (sub, c), 1)

    def body(ci, carry):
        sl = pl.ds(pl.multiple_of(ci * c, c), c)
        q = q_ref[sl, :]
        k = k_ref[sl, :]
        g = g_ref[sl, :]
        vb = v_ref[sl, :]
        g1, g2, g3 = _split3(g)
        cum = _dot(tril, g1) + _dot(tril, g2) + _dot(tril, g3)
        st = st_ref[...]
        o_inter = _dot_nt((q * jnp.exp(cum)).astype(BF16), st.astype(BF16))
        outs = []
        for i in range(c // sub):
            lo, hi = i * sub, (i + 1) * sub
            mid = lo + sub // 2
            r = cum[mid:mid + 1, :]
            qt = (q[lo:hi] * jnp.exp(cum[lo:hi] - r)).astype(BF16)
            kt = (k * jnp.exp(jnp.where(rowk < hi, r - cum, -jnp.inf))).astype(BF16)
            att = jnp.where(scol <= srow + lo, _dot_nt(qt, kt), 0.0)
            outs.append(_dot(att.astype(BF16), vb))
        o = o_inter + jnp.concatenate(outs, axis=0)
        last = cum[c - 1:c, :]
        kd = (k * jnp.exp(last - cum)).astype(BF16)
        st_ref[...] = st * jnp.exp(last) + _dot_tn(vb, kd)
        o = o * og_ref[sl, :]
        o_ref[sl, :] = _head_rmsnorm(o, nw_ref[...]).astype(BF16)
        return carry

    lax.fori_loop(0, seq // c, body, 0)
    s_ref[...] = st_ref[...].T


def _hgrn_prompt(hq, hk, hg, hv, sog, norm_w, b, s):
    blk = pl.BlockSpec((s, HEAD_W), lambda bi, h: (bi, h))
    return pl.pallas_call(
        functools.partial(_hgrn_kernel, seq=s),
        grid=(b, N_HEADS),
        in_specs=[blk, blk, blk, blk, blk, pl.BlockSpec((1, HEAD_W), lambda bi, h: (0, 0))],
        out_specs=[blk, pl.BlockSpec((None, None, HEAD_W, HEAD_W), lambda bi, h: (bi, h, 0, 0))],
        out_shape=[jax.ShapeDtypeStruct((b * s, D_MODEL), BF16),
                   jax.ShapeDtypeStruct((b, N_HEADS, HEAD_W, HEAD_W), F32)],
        scratch_shapes=[pltpu.VMEM((HEAD_W, HEAD_W), F32)],
        compiler_params=_cparams("parallel", "parallel"),
        name="hgrn_prompt",
    )(hq, hk, hg, hv, sog, norm_w)


def _decode_kernel(pt_ref, lam_ref, sw_ref, q_ref, kn_ref, vn_ref, b0_ref, bias_ref, ck_ref, cv_ref, o_ref,
                   kbuf, vbuf, ksem, vsem, m_ref, l_ref, acc_ref, *, group, steps_per_seq, lam_init):
    step = pl.program_id(0)
    j = step % steps_per_seq
    slot = step % 2

    def page_copies(s, sl):
        n_, j_ = s // steps_per_seq, s % steps_per_seq
        cps = []
        for g in range(group):
            page = pt_ref[n_, j_ * group + g]
            cps.append(pltpu.make_async_copy(ck_ref.at[page], kbuf.at[sl, g], ksem.at[sl]))
            cps.append(pltpu.make_async_copy(cv_ref.at[page], vbuf.at[sl, g], vsem.at[sl]))
        return cps

    @pl.when(step == 0)
    def _():
        for cp in page_copies(step, slot):
            cp.start()

    @pl.when(step + 1 < pl.num_programs(0))
    def _():
        for cp in page_copies(step + 1, 1 - slot):
            cp.start()

    for cp in page_copies(step, slot):
        cp.wait()

    @pl.when(j == 0)
    def _():
        m_ref[...] = jnp.full(m_ref.shape, NEG, F32)
        l_ref[...] = jnp.zeros_like(l_ref)
        acc_ref[...] = jnp.zeros_like(acc_ref)

    q = q_ref[...].astype(F32)
    lane = lax.broadcasted_iota(I32, q.shape, 1)
    q1 = jnp.where(lane < A_HD, q, 0.0)
    q2 = jnp.where(lane >= A_HD, q, 0.0)
    a = jnp.concatenate([q1, q2], axis=0).astype(BF16)
    m, l, acc = m_ref[...], l_ref[...], acc_ref[...]
    for g in range(group):
        kb = kbuf[slot, g].reshape(PAGE * N_HEADS, HEAD_W).astype(BF16)
        vb = vbuf[slot, g].reshape(PAGE * N_HEADS, HEAD_W).astype(BF16)
        s = _dot_nt(a, kb) + bias_ref[:, g * PAGE * N_HEADS:(g + 1) * PAGE * N_HEADS]
        mn = jnp.maximum(m, jnp.max(s, axis=-1, keepdims=True))
        p = jnp.exp(s - mn)
        al = jnp.exp(m - mn)
        l = al * l + jnp.sum(p, axis=-1, keepdims=True)
        acc = al * acc + _dot(p.astype(BF16), vb)
        m = mn
    m_ref[...], l_ref[...], acc_ref[...] = m, l, acc

    @pl.when(j == steps_per_seq - 1)
    def _():
        kn = kn_ref[...]
        vn = vn_ref[...]
        prod = q * kn
        s1 = jnp.sum(jnp.where(lane < A_HD, prod, 0.0), axis=-1, keepdims=True) + b0_ref[...]
        s2 = jnp.sum(jnp.where(lane >= A_HD, prod, 0.0), axis=-1, keepdims=True) + b0_ref[...]
        outs = []
        for c, sn in enumerate((s1, s2)):
            mc = m[c * N_HEADS:(c + 1) * N_HEADS]
            lc = l[c * N_HEADS:(c + 1) * N_HEADS]
            ac = acc[c * N_HEADS:(c + 1) * N_HEADS]
            mn = jnp.maximum(mc, sn)
            pn = jnp.exp(sn - mn)
            al = jnp.exp(mc - mn)
            outs.append((al * ac + pn * vn) / (al * lc + pn))
        lam = _lam(lam_ref, lam_init)
        o = outs[0] - lam * outs[1]
        o_ref[...] = _head_rmsnorm(o, sw_ref[...]) * (1.0 - lam_init)


def _attn_sample(page_table, lamv, subln_w, q_s, k_new, v_new, bias0, bias_tab, cache_k, cache_v,
                 group, lam_init):
    n_seq, n_pages = page_table.shape
    sps = n_pages // group
    head_blk = pl.BlockSpec((None, N_HEADS, HEAD_W), lambda st, pt: (st // sps, 0, 0))
    fixed = lambda st, pt: (0, 0)
    cols = group * PAGE * N_HEADS
    page_buf = pltpu.VMEM((2, group, PAGE, N_HEADS, HEAD_W), F32)
    grid_spec = pltpu.PrefetchScalarGridSpec(
        num_scalar_prefetch=1,
        grid=(n_seq * sps,),
        in_specs=[pl.BlockSpec((4, A_HD), fixed),
                  pl.BlockSpec((1, HEAD_W), fixed),
                  head_blk, head_blk, head_blk,
                  pl.BlockSpec((N_HEADS, 1), fixed),
                  pl.BlockSpec((2 * N_HEADS, cols), lambda st, pt: (0, st % sps)),
                  pl.BlockSpec(memory_space=pl.ANY),
                  pl.BlockSpec(memory_space=pl.ANY)],
        out_specs=head_blk,
        scratch_shapes=[page_buf, page_buf, pltpu.SemaphoreType.DMA((2,)), pltpu.SemaphoreType.DMA((2,)),
                        pltpu.VMEM((2 * N_HEADS, 1), F32), pltpu.VMEM((2 * N_HEADS, 1), F32),
                        pltpu.VMEM((2 * N_HEADS, HEAD_W), F32)],
    )
    return pl.pallas_call(
        functools.partial(_decode_kernel, group=group, steps_per_seq=sps, lam_init=lam_init),
        grid_spec=grid_spec,
        out_shape=jax.ShapeDtypeStruct((n_seq, N_HEADS, HEAD_W), F32),
        compiler_params=_cparams("arbitrary"),
        name="attn_sample",
    )(page_table, lamv, subln_w, q_s, k_new, v_new, bias0, bias_tab, cache_k, cache_v)


def _hgrn_step_kernel(q_ref, k_ref, g_ref, v_ref, og_ref, nw_ref, s_ref, o_ref, so_ref):
    f = jnp.exp(g_ref[...])
    stack = jnp.concatenate([f, k_ref[...], q_ref[...],
                             jnp.zeros((HEAD_W - 3 * N_HEADS, HEAD_W), F32)], axis=0)
    cols = stack.T
    v = v_ref[...].astype(F32)
    outs = []
    for h in range(N_HEADS):
        fc = cols[:, h:h + 1]
        kc = cols[:, N_HEADS + h:N_HEADS + h + 1]
        qc = cols[:, 2 * N_HEADS + h:2 * N_HEADS + h + 1]
        s_new = fc * s_ref[h] + kc * v[h:h + 1, :]
        so_ref[h] = s_new
        outs.append(jnp.sum(qc * s_new, axis=0, keepdims=True))
    o = jnp.concatenate(outs, axis=0) * og_ref[...]
    o_ref[...] = _head_rmsnorm(o, nw_ref[...])


def _hgrn_sample(hq, hk, hg, hv, sog, norm_w, state):
    n_seq = state.shape[0]
    head_blk = pl.BlockSpec((None, N_HEADS, HEAD_W), lambda n: (n, 0, 0))
    st_blk = pl.BlockSpec((None, N_HEADS, HEAD_W, HEAD_W), lambda n: (n, 0, 0, 0))
    return pl.pallas_call(
        _hgrn_step_kernel,
        grid=(n_seq,),
        in_specs=[head_blk] * 5 + [pl.BlockSpec((1, HEAD_W), lambda n: (0, 0)), st_blk],
        out_specs=[head_blk, st_blk],
        out_shape=[jax.ShapeDtypeStruct((n_seq, N_HEADS, HEAD_W), F32),
                   jax.ShapeDtypeStruct(state.shape, F32)],
        compiler_params=_cparams("parallel"),
        name="hgrn_sample",
    )(hq, hk, hg, hv, sog, norm_w, state)


def _postmix_kernel(xp_ref, oap_ref, obp_ref, gap_ref, gbp_ref, xs_ref, oas_ref, obs_ref, gas_ref, gbs_ref,
                    wpa_ref, wpb_ref, wo_ref, n2_ref, wr_ref, br_ref,
                    x1_ref, h2_ref, idx_ref, gate_ref, rank_ref, cnt_ref, carry_ref, *, n_prompt_tiles):
    i = pl.program_id(0)
    is_s = i >= n_prompt_tiles

    @pl.when(i == 0)
    def _():
        carry_ref[...] = jnp.zeros_like(carry_ref)

    def pick(p_ref, s_ref):
        return jnp.where(is_s, s_ref[...], p_ref[...])

    x = pick(xp_ref, xs_ref)
    oa = pick(oap_ref, oas_ref).astype(BF16)
    ob = pick(obp_ref, obs_ref).astype(BF16)
    m = pick(gap_ref, gas_ref) * _dot(oa, wpa_ref[...]) + pick(gbp_ref, gbs_ref) * _dot(ob, wpb_ref[...])
    x1 = x + _dot(m.astype(BF16), wo_ref[...])
    x1_ref[...] = x1
    h2 = x1 * lax.rsqrt(jnp.mean(x1 * x1, axis=-1, keepdims=True) + EPS) * n2_ref[...]
    h2_ref[...] = h2
    logits = _dot(h2.astype(BF16), wr_ref[...]) + br_ref[...]
    tm = logits.shape[0]
    lane = lax.broadcasted_iota(I32, logits.shape, 1)
    logits = jnp.where(lane < N_EXPERTS, logits, -jnp.inf)
    vals, idxs = [], []
    for _ in range(TOP_K):
        mx = jnp.max(logits, axis=-1, keepdims=True)
        ix = jnp.min(jnp.where(logits == mx, lane, LANES), axis=-1, keepdims=True)
        vals.append(mx)
        idxs.append(ix)
        logits = jnp.where(lane == ix, -jnp.inf, logits)
    es = [jnp.exp(v - vals[0]) for v in vals]
    den = es[0] + es[1] + es[2] + es[3]
    onehots = [lane == ix for ix in idxs]
    chosen = jnp.where(onehots[0] | onehots[1] | onehots[2] | onehots[3], 1.0, 0.0)
    row = lax.broadcasted_iota(I32, (tm, tm), 0)
    col = lax.broadcasted_iota(I32, (tm, tm), 1)
    strict = jnp.where(row > col, 1.0, 0.0).astype(BF16)
    before = _dot(strict, chosen.astype(BF16)) + carry_ref[...]
    idx_out = jnp.zeros(logits.shape, I32)
    gate_out = jnp.zeros(logits.shape, F32)
    rank_out = jnp.zeros(logits.shape, F32)
    for jj in range(TOP_K):
        rk = jnp.sum(jnp.where(onehots[jj], before, 0.0), axis=-1, keepdims=True)
        idx_out = jnp.where(lane == jj, idxs[jj], idx_out)
        gate_out = jnp.where(lane == jj, es[jj] / den, gate_out)
        rank_out = jnp.where(lane == jj, rk, rank_out)
    idx_ref[...] = idx_out
    gate_ref[...] = gate_out
    rank_ref[...] = rank_out.astype(I32)
    carry = carry_ref[...] + jnp.sum(chosen, axis=0, keepdims=True)
    carry_ref[...] = carry
    cnt_ref[...] = carry.astype(I32)


def _postmix(prompt_in, sample_in, w_pa, w_pb, w_o, norm2_w, w_r, b_r, tm):
    n_p = prompt_in[0].shape[0]
    n_s = sample_in[0].shape[0]
    assert n_s == tm and n_p % tm == 0
    npt = n_p // tm
    n = n_p + n_s
    p_blk = pl.BlockSpec((tm, D_MODEL), lambda i: (jnp.minimum(i, npt - 1), 0))
    s_blk = pl.BlockSpec((tm, D_MODEL), lambda i: (0, 0))
    fixed = lambda i: (0, 0)
    w_blk = pl.BlockSpec((D_MODEL, D_MODEL), fixed)
    vec = pl.BlockSpec((1, D_MODEL), fixed)
    row = lambda i: (i, 0)
    return pl.pallas_call(
        functools.partial(_postmix_kernel, n_prompt_tiles=npt),
        grid=(npt + 1,),
        in_specs=[p_blk] * 5 + [s_blk] * 5 + [w_blk, w_blk, w_blk, vec,
                                              pl.BlockSpec((D_MODEL, LANES), fixed),
                                              pl.BlockSpec((1, LANES), fixed)],
        out_specs=[pl.BlockSpec((tm, D_MODEL), row), pl.BlockSpec((tm, D_MODEL), row),
                   pl.BlockSpec((tm, LANES), row), pl.BlockSpec((tm, LANES), row),
                   pl.BlockSpec((tm, LANES), row), pl.BlockSpec((1, LANES), fixed)],
        out_shape=[jax.ShapeDtypeStruct((n, D_MODEL), F32), jax.ShapeDtypeStruct((n, D_MODEL), F32),
                   jax.ShapeDtypeStruct((n, LANES), I32), jax.ShapeDtypeStruct((n, LANES), F32),
                   jax.ShapeDtypeStruct((n, LANES), I32), jax.ShapeDtypeStruct((1, LANES), I32)],
        scratch_shapes=[pltpu.VMEM((1, LANES), F32)],
        compiler_params=_cparams("arbitrary"),
        name="postmix",
    )(*prompt_in, *sample_in, w_pa, w_pb, w_o, norm2_w, w_r, b_r)


def _row_copy(src, s_row, dst, d_row, sem):
    return pltpu.make_async_copy(src.at[pl.ds(s_row, 1)], dst.at[pl.ds(d_row, 1)], sem)


def _dispatch_kernel(dest_ref, h2_ref, xs_ref, sem, *, tm):
    base = pl.program_id(0) * tm

    def start(t, c):
        for j in range(TOP_K):
            _row_copy(h2_ref, base + t, xs_ref, dest_ref[0, t * TOP_K + j], sem).start()
        return c

    def wait(t, c):
        for j in range(TOP_K):
            _row_copy(h2_ref, base + t, xs_ref, dest_ref[0, t * TOP_K + j], sem).wait()
        return c

    lax.fori_loop(0, tm, start, 0)
    lax.fori_loop(0, tm, wait, 0)


def _dispatch(dest, h2, n_rows, tm):
    n = h2.shape[0]
    nt = n // tm
    return pl.pallas_call(
        functools.partial(_dispatch_kernel, tm=tm),
        grid=(nt,),
        in_specs=[pl.BlockSpec((None, 1, tm * TOP_K), lambda i: (i, 0, 0), memory_space=pltpu.SMEM),
                  pl.BlockSpec(memory_space=pl.ANY)],
        out_specs=pl.BlockSpec(memory_space=pl.ANY),
        out_shape=jax.ShapeDtypeStruct((n_rows, D_MODEL), F32),
        scratch_shapes=[pltpu.SemaphoreType.DMA(())],
        compiler_params=_cparams("arbitrary"),
        name="moe_dispatch",
    )(dest.reshape(nt, 1, tm * TOP_K), h2)


def _expert_kernel(te_ref, tv_ref, xs_ref, w1g_ref, w1l_ref, b1g_ref, b1l_ref, w2_ref, b2_ref, ys_ref):
    r = pl.program_id(0)
    nvalid = tv_ref[r]
    rows = lax.broadcasted_iota(I32, xs_ref.shape, 0)
    x = jnp.where(rows < nvalid, xs_ref[...], 0.0).astype(BF16)
    glu = _dot(x, w1g_ref[...]) + b1g_ref[...]
    lin = _dot(x, w1l_ref[...]) + b1l_ref[...]
    glu = jnp.minimum(glu, SWIGLU_LIMIT)
    lin = jnp.clip(lin, -SWIGLU_LIMIT, SWIGLU_LIMIT)
    act = glu * _sigmoid(SWIGLU_ALPHA * glu) * (lin + 1.0)
    ys_ref[...] = _dot(act.astype(BF16), w2_ref[...]) + b2_ref[...]


def _experts(tile_expert, tile_valid, xs, w1g, w1l, b1g, b1l, w2, b2, tr):
    n_rows = xs.shape[0]
    d_ff = w1g.shape[-1]
    e_map3 = lambda r, te, tv: (te[r], 0, 0)
    grid_spec = pltpu.PrefetchScalarGridSpec(
        num_scalar_prefetch=2,
        grid=(n_rows // tr,),
        in_specs=[pl.BlockSpec((tr, D_MODEL), lambda r, te, tv: (r, 0)),
                  pl.BlockSpec((None, D_MODEL, d_ff), e_map3),
                  pl.BlockSpec((None, D_MODEL, d_ff), e_map3),
                  pl.BlockSpec((None, 1, d_ff), e_map3),
                  pl.BlockSpec((None, 1, d_ff), e_map3),
                  pl.BlockSpec((None, d_ff, D_MODEL), e_map3),
                  pl.BlockSpec((None, 1, D_MODEL), e_map3)],
        out_specs=pl.BlockSpec((tr, D_MODEL), lambda r, te, tv: (r, 0)),
    )
    return pl.pallas_call(
        _expert_kernel,
        grid_spec=grid_spec,
        out_shape=jax.ShapeDtypeStruct((n_rows, D_MODEL), F32),
        compiler_params=_cparams("arbitrary"),
        name="moe_experts",
    )(tile_expert, tile_valid, xs, w1g, w1l, b1g, b1l, w2, b2)


def _combine_kernel(dest_ref, ys_ref, x1_ref, gate_ref, fw_ref, yp_ref, ysm_ref, buf_ref, sem,
                    *, tm, n_prompt_tiles):
    i = pl.program_id(0)

    def start(t, c):
        for j in range(TOP_K):
            pltpu.make_async_copy(ys_ref.at[pl.ds(dest_ref[0, t * TOP_K + j], 1)],
                                  buf_ref.at[j, pl.ds(t, 1)], sem).start()
        return c

    def wait(t, c):
        for j in range(TOP_K):
            pltpu.make_async_copy(ys_ref.at[pl.ds(dest_ref[0, t * TOP_K + j], 1)],
                                  buf_ref.at[j, pl.ds(t, 1)], sem).wait()
        return c

    lax.fori_loop(0, tm, start, 0)
    lax.fori_loop(0, tm, wait, 0)
    gate = gate_ref[...]
    out = x1_ref[...]
    for j in range(TOP_K):
        out = out + gate[:, j:j + 1] * buf_ref[j]
    y = out * lax.rsqrt(jnp.mean(out * out, axis=-1, keepdims=True) + EPS) * fw_ref[...]

    @pl.when(i < n_prompt_tiles)
    def _():
        yp_ref[...] = y

    @pl.when(i >= n_prompt_tiles)
    def _():
        ysm_ref[...] = y


def _combine(dest, ys, x1, gate, final_w, n_p, n_s, tm):
    n = n_p + n_s
    nt = n // tm
    npt = n_p // tm
    row = lambda i: (i, 0)
    return pl.pallas_call(
        functools.partial(_combine_kernel, tm=tm, n_prompt_tiles=npt),
        grid=(nt,),
        in_specs=[pl.BlockSpec((None, 1, tm * TOP_K), lambda i: (i, 0, 0), memory_space=pltpu.SMEM),
                  pl.BlockSpec(memory_space=pl.ANY),
                  pl.BlockSpec((tm, D_MODEL), row),
                  pl.BlockSpec((tm, LANES), row),
                  pl.BlockSpec((1, D_MODEL), lambda i: (0, 0))],
        out_specs=[pl.BlockSpec((tm, D_MODEL), lambda i: (jnp.minimum(i, npt - 1), 0)),
                   pl.BlockSpec((tm, D_MODEL), lambda i: (0, 0))],
        out_shape=[jax.ShapeDtypeStruct((n_p, D_MODEL), F32), jax.ShapeDtypeStruct((n_s, D_MODEL), F32)],
        scratch_shapes=[pltpu.VMEM((TOP_K, tm, D_MODEL), F32), pltpu.SemaphoreType.DMA(())],
        compiler_params=_cparams("arbitrary"),
        name="moe_combine",
    )(dest.reshape(nt, 1, tm * TOP_K), ys, x1, gate, final_w)


def _t5_bucket_np(n):
    n = np.maximum(np.asarray(n, np.int64), 0)
    max_exact = N_BUCKETS // 2
    nf = np.maximum(n, max_exact).astype(np.float32)
    large = max_exact + (np.log(nf / np.float32(max_exact)) / np.float32(math.log(MAX_DISTANCE / max_exact))
                         * np.float32(N_BUCKETS - max_exact)).astype(np.int32)
    return np.where(n < max_exact, n, np.minimum(large, N_BUCKETS - 1)).astype(np.int32)


def _prompt_bias_tiles(rel_bias, tile):
    assert tile >= MAX_DISTANCE
    far = N_BUCKETS - 1
    ql = np.arange(tile)[:, None]
    kl = np.arange(tile)[None, :]
    rel0 = ql - kl
    rel1 = tile + ql - kl
    assert (_t5_bucket_np(np.arange(tile + 1, 4 * tile)) == far).all()
    shifted = (rel_bias - rel_bias[far:far + 1]).T
    t0 = jnp.where(rel0 >= 0, shifted[:, _t5_bucket_np(rel0)], NEG)
    t1 = shifted[:, _t5_bucket_np(rel1)]
    return jnp.stack([t0, t1, jnp.zeros_like(t1)], axis=1).astype(F32)


def _sample_bias_table(rel_bias, past):
    far = N_BUCKETS - 1
    shifted = (rel_bias - rel_bias[far:far + 1]).T
    bucket = _t5_bucket_np(past - np.arange(past))
    per_pos = shifted[:, bucket]
    same = np.eye(N_HEADS, dtype=bool)[:, None, :]
    tab = jnp.where(same, per_pos[:, :, None], NEG).reshape(N_HEADS, past * N_HEADS)
    bias0 = shifted[:, _t5_bucket_np(0)].reshape(N_HEADS, 1)
    return jnp.concatenate([tab, tab], axis=0).astype(F32), bias0.astype(F32)


def kernel(x_prompt, x_sample, cache_k, cache_v, state_hgrn, page_table, norm1_w, w_in, lambda_q1, lambda_k1,
           lambda_q2, lambda_k2, subln_w, lb_logits, hgrn_norm_w, rel_bias, w_pa, w_pb, w_o, norm2_w,
           w_router, b_router, w1, b1, w2, b2, final_norm_w):
    depth = w_in.shape[0]
    assert depth == 1
    l = 0
    b, s, d = x_prompt.shape
    n_seq, t_new, _ = x_sample.shape
    assert d == D_MODEL and t_new == 1
    n_pages = page_table.shape[1]
    past = n_pages * PAGE
    n_p = b * s
    lam_init = 0.8 - 0.6 * math.exp(-0.3 * l)

    lower_bounds = jnp.cumsum(jax.nn.softmax(lb_logits.astype(F32), axis=0), axis=0)
    lb = lower_bounds[l].reshape(1, D_MODEL)
    lamv = jnp.stack([lambda_q1[l], lambda_k1[l], lambda_q2[l], lambda_k2[l]]).astype(F32)
    n1 = norm1_w[l].reshape(1, D_MODEL)
    w_in_bf = w_in[l].astype(BF16)
    sw = subln_w[l].reshape(1, HEAD_W)
    hw = hgrn_norm_w[l].reshape(1, HEAD_W)

    xp = x_prompt.reshape(n_p, D_MODEL)
    qa, ka, va, hq, hk, hg, hv, sog, sga, sgb = _inproj(xp, n1, lb, w_in_bf, 256)
    tile = min(256, s)
    oa_p = _attn_prompt(lamv, sw, qa, ka, va, _prompt_bias_tiles(rel_bias, tile), b, s, tile, lam_init)
    ob_p, st_p = _hgrn_prompt(hq, hk, hg, hv, sog, hw, b, s)

    xs = x_sample.reshape(n_seq, D_MODEL)
    qa_s, ka_s, va_s, hq_s, hk_s, hg_s, hv_s, sog_s, sga_s, sgb_s = _inproj(xs, n1, lb, w_in_bf, n_seq)
    heads = lambda a: a.reshape(n_seq, N_HEADS, HEAD_W)
    bias_tab, bias0 = _sample_bias_table(rel_bias, past)
    group = max(g for g in (8, 4, 2, 1) if n_pages % g == 0)
    oa_s = _attn_sample(page_table, lamv, sw, heads(qa_s), heads(ka_s), heads(va_s), bias0, bias_tab,
                        cache_k.reshape(cache_k.shape[1:]), cache_v.reshape(cache_v.shape[1:]), group, lam_init)
    ob_s, st_s = _hgrn_sample(heads(hq_s), heads(hk_s), heads(hg_s), heads(hv_s), heads(sog_s), hw,
                              state_hgrn.reshape(state_hgrn.shape[1:]).astype(F32))

    w_r = jnp.pad(w_router[l], ((0, 0), (0, LANES - N_EXPERTS))).astype(BF16)
    b_r = jnp.pad(b_router[l].astype(F32), (0, LANES - N_EXPERTS)).reshape(1, LANES)
    x1, h2, idx, gate, rank, cnt = _postmix(
        (xp, oa_p, ob_p, sga, sgb),
        (xs, oa_s.reshape(n_seq, D_MODEL), ob_s.reshape(n_seq, D_MODEL), sga_s, sgb_s),
        w_pa[l].astype(BF16), w_pb[l].astype(BF16), w_o[l].astype(BF16), norm2_w[l].reshape(1, D_MODEL),
        w_r, b_r, TOK_TILE)

    n = n_p + n_seq
    counts = cnt[0, :N_EXPERTS]
    padded = (counts + ROW_TILE - 1) // ROW_TILE * ROW_TILE
    ends = jnp.cumsum(padded)
    starts = ends - padded
    n_tiles = (n * TOP_K) // ROW_TILE + N_EXPERTS
    tile_row = jnp.arange(n_tiles, dtype=I32) * ROW_TILE
    tile_expert = jnp.minimum(jnp.sum((tile_row[:, None] >= ends[None, :]).astype(I32), axis=1), N_EXPERTS - 1)
    tile_valid = jnp.clip(counts[tile_expert] - (tile_row - starts[tile_expert]), 0, ROW_TILE).astype(I32)
    dest = (starts[idx[:, :TOP_K]] + rank[:, :TOP_K]).astype(I32)

    xs_sorted = _dispatch(dest, h2, n_tiles * ROW_TILE, TOK_TILE)
    w1l_ = w1[l]
    w1g = w1l_[..., 0::2].astype(BF16)
    w1l = w1l_[..., 1::2].astype(BF16)
    b1g = b1[l][:, None, 0::2]
    b1l = b1[l][:, None, 1::2]
    ys = _experts(tile_expert, tile_valid, xs_sorted, w1g, w1l, b1g, b1l, w2[l].astype(BF16),
                  b2[l][:, None, :], ROW_TILE)
    y_p, y_s = _combine(dest, ys, x1, gate, final_norm_w.reshape(1, D_MODEL), n_p, n_seq, TOK_TILE)

    kv_p = lambda a: a.reshape(1, b, s, N_HEADS, HEAD_W)
    kv_s = lambda a: a.reshape(1, n_seq, 1, N_HEADS, HEAD_W)
    return (y_p.reshape(b, s, D_MODEL), y_s.reshape(n_seq, 1, D_MODEL), kv_p(ka), kv_p(va),
            st_p[None].astype(state_hgrn.dtype), kv_s(ka_s), kv_s(va_s), st_s[None].astype(state_hgrn.dtype))
```

```python
import functools
import math

import numpy as np
import jax
import jax.numpy as jnp
from jax import lax
from jax.experimental import pallas as pl
from jax.experimental.pallas import tpu as pltpu

F32 = jnp.float32
BF16 = jnp.bfloat16
I32 = jnp.int32

D_MODEL = 1024
N_HEADS = 8
HEAD_W = 128
A_HD = 64
A_SCALE = A_HD ** -0.5
PAGE = 128
N_BUCKETS = 32
MAX_DISTANCE = 128
N_EXPERTS = 32
TOP_K = 4
SWIGLU_ALPHA = 1.702
SWIGLU_LIMIT = 7.0
EPS = 1e-5
NEG = -1e30

LANES = 128
VMEM_LIMIT = 56 * 1024 * 1024

ATTN_TILE = 512
TOK_TILE = 128
ROW_TILE = 256
HGRN_CHUNK = 64
HGRN_SUB = 16
HGRN_TIME_BLOCK = 512
DECODE_SLOTS = 3


def _cparams(*sem):
    return pltpu.CompilerParams(dimension_semantics=sem, vmem_limit_bytes=VMEM_LIMIT)


def _sigmoid(x):
    return 1.0 / (1.0 + jnp.exp(-x))


def _dot(a, b):
    return jnp.dot(a, b, preferred_element_type=F32)


def _dot_nt(a, b):
    return lax.dot_general(a, b, (((1,), (1,)), ((), ())), preferred_element_type=F32)


def _dot_tn(a, b):
    return lax.dot_general(a, b, (((0,), (0,)), ((), ())), preferred_element_type=F32)


def _lam(lam_ref, lam_init):
    lv = lam_ref[...]
    e1 = jnp.exp(jnp.sum(lv[0:1] * lv[1:2], axis=-1, keepdims=True))
    e2 = jnp.exp(jnp.sum(lv[2:3] * lv[3:4], axis=-1, keepdims=True))
    return e1 - e2 + lam_init


def _head_rmsnorm(o, w):
    return o * lax.rsqrt(jnp.mean(o * o, axis=-1, keepdims=True) + EPS) * w


def _inproj_kernel(x_ref, nw_ref, lb_ref, w_ref, qa_ref, ka_ref, va_ref, hq_ref, hk_ref, hg_ref,
                   hv_ref, sog_ref, sga_ref, sgb_ref):
    x = x_ref[...]
    h = x * lax.rsqrt(jnp.mean(x * x, axis=-1, keepdims=True) + EPS) * nw_ref[...]
    hb = h.astype(BF16)

    def proj(s):
        return _dot(hb, w_ref[:, s * D_MODEL:(s + 1) * D_MODEL])

    qa_ref[...] = (proj(0) * A_SCALE).astype(BF16)
    ka_ref[...] = proj(1)
    va_ref[...] = proj(2)
    qb = proj(3)
    hq_ref[...] = qb * _sigmoid(qb)
    lb = lb_ref[...]
    f = lb + (1.0 - lb) * _sigmoid(proj(4))
    hk_ref[...] = 1.0 - f
    hg_ref[...] = jnp.log(f)
    hv_ref[...] = proj(5).astype(BF16)
    sog_ref[...] = _sigmoid(proj(6))
    sga_ref[...] = _sigmoid(proj(7))
    sgb_ref[...] = _sigmoid(proj(8))


def _inproj(x, norm_w, lb, w_in_bf, tm):
    n = x.shape[0]
    row = lambda i: (i, 0)
    fixed = lambda i: (0, 0)
    f32_out = jax.ShapeDtypeStruct((n, D_MODEL), F32)
    bf_out = jax.ShapeDtypeStruct((n, D_MODEL), BF16)
    blk = pl.BlockSpec((tm, D_MODEL), row)
    return pl.pallas_call(
        _inproj_kernel,
        grid=(n // tm,),
        in_specs=[blk,
                  pl.BlockSpec((1, D_MODEL), fixed),
                  pl.BlockSpec((1, D_MODEL), fixed),
                  pl.BlockSpec(w_in_bf.shape, fixed, pipeline_mode=pl.Buffered(1))],
        out_specs=[blk] * 10,
        out_shape=[bf_out, f32_out, f32_out, f32_out, f32_out, f32_out, bf_out, f32_out, f32_out, f32_out],
        compiler_params=_cparams("parallel"),
        name="inproj",
    )(x, norm_w, lb, w_in_bf)


def _attn_kernel(lam_ref, sw_ref, q_ref, k_ref, v_ref, bias_ref, o_ref, q2_ref, m_ref, l_ref, acc_ref,
                 *, tile, lam_init):
    qi = pl.program_id(2)
    q = q_ref[...]
    lane = lax.broadcasted_iota(I32, q.shape, 1)
    q2_ref[:tile] = jnp.where(lane < A_HD, q, jnp.zeros_like(q))
    q2_ref[tile:] = jnp.where(lane >= A_HD, q, jnp.zeros_like(q))
    m_ref[...] = jnp.full(m_ref.shape, NEG, F32)
    l_ref[...] = jnp.zeros_like(l_ref)
    acc_ref[...] = jnp.zeros_like(acc_ref)

    def body(kj, carry):
        sl = pl.ds(pl.multiple_of(kj * tile, tile), tile)
        kb = k_ref[sl, :].astype(BF16)
        vb = v_ref[sl, :].astype(BF16)
        bias = bias_ref[jnp.minimum(qi - kj, 2)]
        s = _dot_nt(kb, q2_ref[...])
        s = jnp.concatenate([s[:, :tile] + bias, s[:, tile:] + bias], axis=1)
        m = m_ref[...]
        mn = jnp.maximum(m, jnp.max(s, axis=0, keepdims=True))
        p = jnp.exp(s - mn)
        al = jnp.exp(m - mn)
        m_ref[...] = mn
        l_ref[...] = al * l_ref[...] + jnp.sum(p, axis=0, keepdims=True)
        acc_ref[...] = al * acc_ref[...] + _dot_tn(vb, p.astype(BF16))
        return carry

    lax.fori_loop(0, qi + 1, body, 0)
    lam = _lam(lam_ref, lam_init)
    on = acc_ref[...] / l_ref[...]
    o = (on[:, :tile] - lam * on[:, tile:]).T
    o_ref[...] = (_head_rmsnorm(o, sw_ref[...]) * (1.0 - lam_init)).astype(BF16)


def _attn_prompt(lamv, subln_w, qa, ka, va, bias_tab, b, s, tile, lam_init):
    nq = s // tile
    return pl.pallas_call(
        functools.partial(_attn_kernel, tile=tile, lam_init=lam_init),
        grid=(b, N_HEADS, nq),
        in_specs=[pl.BlockSpec((4, A_HD), lambda bi, h, qi: (0, 0)),
                  pl.BlockSpec((1, HEAD_W), lambda bi, h, qi: (0, 0)),
                  pl.BlockSpec((tile, HEAD_W), lambda bi, h, qi: (bi * nq + qi, h)),
                  pl.BlockSpec((s, HEAD_W), lambda bi, h, qi: (bi, h)),
                  pl.BlockSpec((s, HEAD_W), lambda bi, h, qi: (bi, h)),
                  pl.BlockSpec((None, 3, tile, tile), lambda bi, h, qi: (h, 0, 0, 0))],
        out_specs=pl.BlockSpec((tile, HEAD_W), lambda bi, h, qi: (bi * nq + qi, h)),
        out_shape=jax.ShapeDtypeStruct((b * s, D_MODEL), BF16),
        scratch_shapes=[pltpu.VMEM((2 * tile, HEAD_W), BF16), pltpu.VMEM((1, 2 * tile), F32),
                        pltpu.VMEM((1, 2 * tile), F32), pltpu.VMEM((HEAD_W, 2 * tile), F32)],
        compiler_params=_cparams("parallel", "parallel", "parallel"),
        name="attn_prompt",
    )(lamv, subln_w, qa, ka, va, bias_tab)


def _split3(g):
    g1 = g.astype(BF16)
    r1 = g - g1.astype(F32)
    g2 = r1.astype(BF16)
    g3 = (r1 - g2.astype(F32)).astype(BF16)
    return g1, g2, g3


def _hgrn_kernel(q_ref, k_ref, g_ref, v_ref, og_ref, nw_ref, o_ref, s_ref, st_ref, *, tb):
    c, sub = HGRN_CHUNK, HGRN_SUB
    t = pl.program_id(1)

    @pl.when(t == 0)
    def _():
        st_ref[...] = jnp.zeros_like(st_ref)

    row = lax.broadcasted_iota(I32, (c, c), 0)
    col = lax.broadcasted_iota(I32, (c, c), 1)
    tril = jnp.where(row >= col, 1.0, 0.0).astype(BF16)
    rowk = lax.broadcasted_iota(I32, (c, D_MODEL), 0)
    srow = lax.broadcasted_iota(I32, (sub, c), 0)
    scol = lax.broadcasted_iota(I32, (sub, c), 1)

    def body(ci, carry):
        sl = pl.ds(pl.multiple_of(ci * c, c), c)
        q = q_ref[sl, :]
        k = k_ref[sl, :]
        vb = v_ref[sl, :]
        g1, g2, g3 = _split3(g_ref[sl, :])
        cum = _dot(tril, g1) + _dot(tril, g2) + _dot(tril, g3)
        qe = (q * jnp.exp(cum)).astype(BF16)
        last = cum[c - 1:c, :]
        kd = (k * jnp.exp(last - cum)).astype(BF16)
        el = jnp.exp(last)
        qts, kts = [], []
        for i in range(c // sub):
            lo, hi = i * sub, (i + 1) * sub
            mid = lo + sub // 2
            r = cum[mid:mid + 1, :]
            qts.append((q[lo:hi] * jnp.exp(cum[lo:hi] - r)).astype(BF16))
            kts.append((k * jnp.exp(jnp.where(rowk < hi, r - cum, -jnp.inf))).astype(BF16))
        og = og_ref[sl, :]
        nw = nw_ref[...]
        sts = [st_ref[h] for h in range(N_HEADS)]
        hss = [slice(h * HEAD_W, (h + 1) * HEAD_W) for h in range(N_HEADS)]
        atts = [[_dot_nt(qts[i][:, hs], kts[i][:, hs]) for i in range(c // sub)] for hs in hss]
        inters = [_dot_nt(qe[:, hs], sts[h].astype(BF16)) for h, hs in enumerate(hss)]
        upds = [_dot_tn(vb[:, hs], kd[:, hs]) for hs in hss]
        o_heads = []
        for h, hs in enumerate(hss):
            outs = [_dot(jnp.where(scol <= srow + i * sub, atts[h][i], 0.0).astype(BF16), vb[:, hs])
                    for i in range(c // sub)]
            o = inters[h] + jnp.concatenate(outs, axis=0)
            o_heads.append(_head_rmsnorm(o * og[:, hs], nw).astype(BF16))
        for h, hs in enumerate(hss):
            st_ref[h] = sts[h] * el[:, hs] + upds[h]
        o_ref[sl, :] = jnp.concatenate(o_heads, axis=1)
        return carry

    lax.fori_loop(0, tb // c, body, 0)

    @pl.when(t == pl.num_programs(1) - 1)
    def _():
        for h in range(N_HEADS):
            s_ref[h] = st_ref[h].T


def _hgrn_prompt(hq, hk, hg, hv, sog, norm_w, b, s, tb):
    nt = s // tb
    blk = pl.BlockSpec((tb, D_MODEL), lambda bi, t: (bi * nt + t, 0))
    return pl.pallas_call(
        functools.partial(_hgrn_kernel, tb=tb),
        grid=(b, nt),
        in_specs=[blk, blk, blk, blk, blk, pl.BlockSpec((1, HEAD_W), lambda bi, t: (0, 0))],
        out_specs=[blk, pl.BlockSpec((None, N_HEADS, HEAD_W, HEAD_W), lambda bi, t: (bi, 0, 0, 0))],
        out_shape=[jax.ShapeDtypeStruct((b * s, D_MODEL), BF16),
                   jax.ShapeDtypeStruct((b, N_HEADS, HEAD_W, HEAD_W), F32)],
        scratch_shapes=[pltpu.VMEM((N_HEADS, HEAD_W, HEAD_W), F32)],
        compiler_params=_cparams("parallel", "arbitrary"),
        name="hgrn_prompt",
    )(hq, hk, hg, hv, sog, norm_w)


def _decode_kernel(pt_ref, lam_ref, sw_ref, q_ref, kn_ref, vn_ref, b0_ref, bias_ref, ck_ref, cv_ref, o_ref,
                   kbuf, vbuf, ksem, vsem, m_ref, l_ref, acc_ref, *, layer, group, steps_per_seq, lam_init):
    step = pl.program_id(0)
    n_steps = pl.num_programs(0)
    j = step % steps_per_seq
    slot = step % DECODE_SLOTS

    def page_copies(s, sl):
        n_, j_ = s // steps_per_seq, s % steps_per_seq
        cps = []
        for g in range(group):
            page = pt_ref[n_, j_ * group + g]
            cps.append(pltpu.make_async_copy(ck_ref.at[layer, page], kbuf.at[sl, g], ksem.at[sl]))
            cps.append(pltpu.make_async_copy(cv_ref.at[layer, page], vbuf.at[sl, g], vsem.at[sl]))
        return cps

    for ahead in range(DECODE_SLOTS - 1):
        @pl.when((step == 0) & (ahead < n_steps))
        def _():
            for cp in page_copies(ahead, ahead):
                cp.start()

    nxt = step + DECODE_SLOTS - 1

    @pl.when(nxt < n_steps)
    def _():
        for cp in page_copies(nxt, nxt % DECODE_SLOTS):
            cp.start()

    for cp in page_copies(step, slot):
        cp.wait()

    @pl.when(j == 0)
    def _():
        m_ref[...] = jnp.full(m_ref.shape, NEG, F32)
        l_ref[...] = jnp.zeros_like(l_ref)
        acc_ref[...] = jnp.zeros_like(acc_ref)

    q = q_ref[...].astype(F32)
    lane = lax.broadcasted_iota(I32, q.shape, 1)
    q1 = jnp.where(lane < A_HD, q, 0.0)
    q2 = jnp.where(lane >= A_HD, q, 0.0)
    a = jnp.concatenate([q1, q2], axis=0).astype(BF16)
    m, l, acc = m_ref[...], l_ref[...], acc_ref[...]
    cols = group * PAGE * N_HEADS
    kb = kbuf[slot].reshape(cols, HEAD_W).astype(BF16)
    vb = vbuf[slot].reshape(cols, HEAD_W).astype(BF16)
    s = _dot_nt(a, kb)
    srow = lax.broadcasted_iota(I32, s.shape, 0)
    scol = lax.broadcasted_iota(I32, s.shape, 1)
    s = jnp.where((srow & (N_HEADS - 1)) == (scol & (N_HEADS - 1)), s, NEG)
    near = jnp.where(j == steps_per_seq - 1, 1.0, 0.0)
    s = jnp.concatenate([s[:, :cols - PAGE * N_HEADS], s[:, cols - PAGE * N_HEADS:] + near * bias_ref[...]], axis=1)
    mn = jnp.maximum(m, jnp.max(s, axis=-1, keepdims=True))
    p = jnp.exp(s - mn)
    al = jnp.exp(m - mn)
    l = al * l + jnp.sum(p, axis=-1, keepdims=True)
    acc = al * acc + _dot(p.astype(BF16), vb)
    m = mn
    m_ref[...], l_ref[...], acc_ref[...] = m, l, acc

    @pl.when(j == steps_per_seq - 1)
    def _():
        kn = kn_ref[...]
        vn = vn_ref[...]
        prod = q * kn
        s1 = jnp.sum(jnp.where(lane < A_HD, prod, 0.0), axis=-1, keepdims=True) + b0_ref[...]
        s2 = jnp.sum(jnp.where(lane >= A_HD, prod, 0.0), axis=-1, keepdims=True) + b0_ref[...]
        outs = []
        for c, sn in enumerate((s1, s2)):
            mc = m[c * N_HEADS:(c + 1) * N_HEADS]
            lc = l[c * N_HEADS:(c + 1) * N_HEADS]
            ac = acc[c * N_HEADS:(c + 1) * N_HEADS]
            mn = jnp.maximum(mc, sn)
            pn = jnp.exp(sn - mn)
            al = jnp.exp(mc - mn)
            outs.append((al * ac + pn * vn) / (al * lc + pn))
        lam = _lam(lam_ref, lam_init)
        o = outs[0] - lam * outs[1]
        o_ref[...] = _head_rmsnorm(o, sw_ref[...]) * (1.0 - lam_init)


def _attn_sample(page_table, lamv, subln_w, q_s, k_new, v_new, bias0, bias_last, cache_k, cache_v,
                 layer, group, lam_init):
    n_seq, n_pages = page_table.shape
    sps = n_pages // group
    head_blk = pl.BlockSpec((None, N_HEADS, HEAD_W), lambda st, pt: (st // sps, 0, 0))
    fixed = lambda st, pt: (0, 0)
    page_buf = pltpu.VMEM((DECODE_SLOTS, group, PAGE, N_HEADS, HEAD_W), F32)
    page_sem = pltpu.SemaphoreType.DMA((DECODE_SLOTS,))
    grid_spec = pltpu.PrefetchScalarGridSpec(
        num_scalar_prefetch=1,
        grid=(n_seq * sps,),
        in_specs=[pl.BlockSpec((4, A_HD), fixed),
                  pl.BlockSpec((1, HEAD_W), fixed),
                  head_blk, head_blk, head_blk,
                  pl.BlockSpec((N_HEADS, 1), fixed),
                  pl.BlockSpec((2 * N_HEADS, PAGE * N_HEADS), fixed),
                  pl.BlockSpec(memory_space=pl.ANY),
                  pl.BlockSpec(memory_space=pl.ANY)],
        out_specs=head_blk,
        scratch_shapes=[page_buf, page_buf, page_sem, page_sem,
                        pltpu.VMEM((2 * N_HEADS, 1), F32), pltpu.VMEM((2 * N_HEADS, 1), F32),
                        pltpu.VMEM((2 * N_HEADS, HEAD_W), F32)],
    )
    return pl.pallas_call(
        functools.partial(_decode_kernel, layer=layer, group=group, steps_per_seq=sps, lam_init=lam_init),
        grid_spec=grid_spec,
        out_shape=jax.ShapeDtypeStruct((n_seq, N_HEADS, HEAD_W), F32),
        compiler_params=_cparams("arbitrary"),
        name="attn_sample",
    )(page_table, lamv, subln_w, q_s, k_new, v_new, bias0, bias_last, cache_k, cache_v)


def _hgrn_step_kernel(q_ref, k_ref, g_ref, v_ref, og_ref, nw_ref, s_ref, o_ref, so_ref):
    f = jnp.exp(g_ref[...])
    stack = jnp.concatenate([f, k_ref[...], q_ref[...],
                             jnp.zeros((HEAD_W - 3 * N_HEADS, HEAD_W), F32)], axis=0)
    cols = stack.T
    v = v_ref[...].astype(F32)
    outs = []
    for h in range(N_HEADS):
        fc = cols[:, h:h + 1]
        kc = cols[:, N_HEADS + h:N_HEADS + h + 1]
        qc = cols[:, 2 * N_HEADS + h:2 * N_HEADS + h + 1]
        s_new = fc * s_ref[h].astype(F32) + kc * v[h:h + 1, :]
        so_ref[h] = s_new.astype(so_ref.dtype)
        outs.append(jnp.sum(qc * s_new, axis=0, keepdims=True))
    o = jnp.concatenate(outs, axis=0) * og_ref[...]
    o_ref[...] = _head_rmsnorm(o, nw_ref[...])


def _hgrn_sample(hq, hk, hg, hv, sog, norm_w, state, layer):
    n_seq = state.shape[1]
    head_blk = pl.BlockSpec((None, N_HEADS, HEAD_W), lambda n: (n, 0, 0))
    st_in = pl.BlockSpec((None, None, N_HEADS, HEAD_W, HEAD_W), lambda n: (layer, n, 0, 0, 0))
    st_out = pl.BlockSpec((None, None, N_HEADS, HEAD_W, HEAD_W), lambda n: (0, n, 0, 0, 0))
    return pl.pallas_call(
        _hgrn_step_kernel,
        grid=(n_seq,),
        in_specs=[head_blk] * 5 + [pl.BlockSpec((1, HEAD_W), lambda n: (0, 0)), st_in],
        out_specs=[head_blk, st_out],
        out_shape=[jax.ShapeDtypeStruct((n_seq, N_HEADS, HEAD_W), F32),
                   jax.ShapeDtypeStruct((1,) + state.shape[1:], state.dtype)],
        compiler_params=_cparams("parallel"),
        name="hgrn_sample",
    )(hq, hk, hg, hv, sog, norm_w, state)


def _postmix_kernel(xp_ref, oap_ref, obp_ref, gap_ref, gbp_ref, xs_ref, oas_ref, obs_ref, gas_ref, gbs_ref,
                    wpa_ref, wpb_ref, wo_ref, n2_ref, wr_ref, br_ref,
                    x1_ref, h2_ref, idx_ref, gate_ref, rank_ref, cnt_ref, carry_ref, *, n_prompt_tiles):
    i = pl.program_id(0)
    is_s = i >= n_prompt_tiles

    @pl.when(i == 0)
    def _():
        carry_ref[...] = jnp.zeros_like(carry_ref)

    def pick(p_ref, s_ref):
        return jnp.where(is_s, s_ref[...], p_ref[...])

    x = pick(xp_ref, xs_ref)
    oa = pick(oap_ref, oas_ref).astype(BF16)
    ob = pick(obp_ref, obs_ref).astype(BF16)
    m = pick(gap_ref, gas_ref) * _dot(oa, wpa_ref[...]) + pick(gbp_ref, gbs_ref) * _dot(ob, wpb_ref[...])
    x1 = x + _dot(m.astype(BF16), wo_ref[...])
    x1_ref[...] = x1
    h2 = x1 * lax.rsqrt(jnp.mean(x1 * x1, axis=-1, keepdims=True) + EPS) * n2_ref[...]
    h2_ref[...] = h2
    logits = _dot(h2.astype(BF16), wr_ref[...]) + br_ref[...]
    tm = logits.shape[0]
    lane = lax.broadcasted_iota(I32, logits.shape, 1)
    logits = jnp.where(lane < N_EXPERTS, logits, -jnp.inf)
    vals, idxs = [], []
    for _ in range(TOP_K):
        mx = jnp.max(logits, axis=-1, keepdims=True)
        ix = jnp.min(jnp.where(logits == mx, lane, LANES), axis=-1, keepdims=True)
        vals.append(mx)
        idxs.append(ix)
        logits = jnp.where(lane == ix, -jnp.inf, logits)
    es = [jnp.exp(v - vals[0]) for v in vals]
    den = es[0] + es[1] + es[2] + es[3]
    onehots = [lane == ix for ix in idxs]
    chosen = jnp.where(onehots[0] | onehots[1] | onehots[2] | onehots[3], 1.0, 0.0)
    row = lax.broadcasted_iota(I32, (tm, tm), 0)
    col = lax.broadcasted_iota(I32, (tm, tm), 1)
    strict = jnp.where(row > col, 1.0, 0.0).astype(BF16)
    before = _dot(strict, chosen.astype(BF16)) + carry_ref[...]
    idx_out = jnp.zeros(logits.shape, I32)
    gate_out = jnp.zeros(logits.shape, F32)
    rank_out = jnp.zeros(logits.shape, F32)
    for jj in range(TOP_K):
        rk = jnp.sum(jnp.where(onehots[jj], before, 0.0), axis=-1, keepdims=True)
        idx_out = jnp.where(lane == jj, idxs[jj], idx_out)
        gate_out = jnp.where(lane == jj, es[jj] / den, gate_out)
        rank_out = jnp.where(lane == jj, rk, rank_out)
    idx_ref[...] = idx_out
    gate_ref[...] = gate_out
    rank_ref[...] = rank_out.astype(I32)
    carry = carry_ref[...] + jnp.sum(chosen, axis=0, keepdims=True)
    carry_ref[...] = carry
    cnt_ref[...] = carry.astype(I32)


def _postmix(prompt_in, sample_in, w_pa, w_pb, w_o, norm2_w, w_r, b_r, tm):
    n_p = prompt_in[0].shape[0]
    n_s = sample_in[0].shape[0]
    assert n_s == tm and n_p % tm == 0
    npt = n_p // tm
    n = n_p + n_s
    p_blk = pl.BlockSpec((tm, D_MODEL), lambda i: (jnp.minimum(i, npt - 1), 0))
    s_blk = pl.BlockSpec((tm, D_MODEL), lambda i: (0, 0))
    fixed = lambda i: (0, 0)
    w_blk = pl.BlockSpec((D_MODEL, D_MODEL), fixed)
    vec = pl.BlockSpec((1, D_MODEL), fixed)
    row = lambda i: (i, 0)
    return pl.pallas_call(
        functools.partial(_postmix_kernel, n_prompt_tiles=npt),
        grid=(npt + 1,),
        in_specs=[p_blk] * 5 + [s_blk] * 5 + [w_blk, w_blk, w_blk, vec,
                                              pl.BlockSpec((D_MODEL, LANES), fixed),
                                              pl.BlockSpec((1, LANES), fixed)],
        out_specs=[pl.BlockSpec((tm, D_MODEL), row), pl.BlockSpec((tm, D_MODEL), row),
                   pl.BlockSpec((tm, LANES), row), pl.BlockSpec((tm, LANES), row),
                   pl.BlockSpec((tm, LANES), row), pl.BlockSpec((1, LANES), fixed)],
        out_shape=[jax.ShapeDtypeStruct((n, D_MODEL), F32), jax.ShapeDtypeStruct((n, D_MODEL), F32),
                   jax.ShapeDtypeStruct((n, LANES), I32), jax.ShapeDtypeStruct((n, LANES), F32),
                   jax.ShapeDtypeStruct((n, LANES), I32), jax.ShapeDtypeStruct((1, LANES), I32)],
        scratch_shapes=[pltpu.VMEM((1, LANES), F32)],
        compiler_params=_cparams("arbitrary"),
        name="postmix",
    )(*prompt_in, *sample_in, w_pa, w_pb, w_o, norm2_w, w_r, b_r)


def _row_copy(src, s_row, dst, d_row, sem):
    return pltpu.make_async_copy(src.at[pl.ds(s_row, 1)], dst.at[pl.ds(d_row, 1)], sem)


def _dispatch_kernel(pad_lo_ref, pad_hi_ref, used_ref, dest_ref, h2_ref, xs_ref, zero_ref, sem, zsem,
                     *, tm, tr, n_tiles):
    i = pl.program_id(0)

    @pl.when(i == 0)
    def _():
        zero_ref[...] = jnp.zeros_like(zero_ref)

    def token_copy(t, j):
        return _row_copy(h2_ref, t, xs_ref, dest_ref[0, t * TOP_K + j], sem)

    def pad_copy(r):
        return _row_copy(zero_ref, 0, xs_ref, r, zsem)

    def tail_copy():
        row0 = pl.multiple_of((used_ref[0] + i) * tr, tr)
        return pltpu.make_async_copy(zero_ref, xs_ref.at[pl.ds(row0, tr)], zsem)

    def start(t, c):
        for j in range(TOP_K):
            token_copy(t, j).start()
        return c

    def wait(t, c):
        for j in range(TOP_K):
            token_copy(t, j).wait()
        return c

    has_pad = i < N_EXPERTS
    e = jnp.minimum(i, N_EXPERTS - 1)
    lo = jnp.where(has_pad, pad_lo_ref[e], 0)
    hi = jnp.where(has_pad, pad_hi_ref[e], 0)
    has_tail = used_ref[0] + i < n_tiles

    lax.fori_loop(0, tm, start, 0)
    lax.fori_loop(lo, hi, lambda r, c: (pad_copy(r).start(), c)[1], 0)

    @pl.when(has_tail)
    def _():
        tail_copy().start()

    lax.fori_loop(0, tm, wait, 0)
    lax.fori_loop(lo, hi, lambda r, c: (pad_copy(r).wait(), c)[1], 0)

    @pl.when(has_tail)
    def _():
        tail_copy().wait()


def _dispatch(pad_lo, pad_hi, used_tiles, dest, h2, n_tiles, tr, tm):
    n = h2.shape[0]
    nt = n // tm
    assert nt >= N_EXPERTS
    grid_spec = pltpu.PrefetchScalarGridSpec(
        num_scalar_prefetch=3,
        grid=(nt,),
        in_specs=[pl.BlockSpec((None, 1, tm * TOP_K), lambda i, *_: (i, 0, 0), memory_space=pltpu.SMEM),
                  pl.BlockSpec((tm, D_MODEL), lambda i, *_: (i, 0))],
        out_specs=pl.BlockSpec(memory_space=pl.ANY),
        scratch_shapes=[pltpu.VMEM((tr, D_MODEL), F32), pltpu.SemaphoreType.DMA(()), pltpu.SemaphoreType.DMA(())],
    )
    return pl.pallas_call(
        functools.partial(_dispatch_kernel, tm=tm, tr=tr, n_tiles=n_tiles),
        grid_spec=grid_spec,
        out_shape=jax.ShapeDtypeStruct((n_tiles * tr, D_MODEL), F32),
        compiler_params=_cparams("arbitrary"),
        name="moe_dispatch",
    )(pad_lo, pad_hi, used_tiles, dest.reshape(nt, 1, tm * TOP_K), h2)


def _expert_kernel(te_ref, tv_ref, xs_ref, w1g_ref, w1l_ref, b1g_ref, b1l_ref, w2_ref, b2_ref, ys_ref):
    r = pl.program_id(0)
    nvalid = tv_ref[r]
    rows = lax.broadcasted_iota(I32, xs_ref.shape, 0)
    x = jnp.where(rows < nvalid, xs_ref[...], 0.0).astype(BF16)
    glu = _dot(x, w1g_ref[...]) + b1g_ref[...]
    lin = _dot(x, w1l_ref[...]) + b1l_ref[...]
    glu = jnp.minimum(glu, SWIGLU_LIMIT)
    lin = jnp.clip(lin, -SWIGLU_LIMIT, SWIGLU_LIMIT)
    act = glu * _sigmoid(SWIGLU_ALPHA * glu) * (lin + 1.0)
    ys_ref[...] = _dot(act.astype(BF16), w2_ref[...]) + b2_ref[...]


def _experts(tile_expert, tile_valid, xs, w1g, w1l, b1g, b1l, w2, b2, tr):
    n_rows = xs.shape[0]
    d_ff = w1g.shape[-1]
    e_map3 = lambda r, te, tv: (te[r], 0, 0)
    grid_spec = pltpu.PrefetchScalarGridSpec(
        num_scalar_prefetch=2,
        grid=(n_rows // tr,),
        in_specs=[pl.BlockSpec((tr, D_MODEL), lambda r, te, tv: (r, 0)),
                  pl.BlockSpec((None, D_MODEL, d_ff), e_map3),
                  pl.BlockSpec((None, D_MODEL, d_ff), e_map3),
                  pl.BlockSpec((None, 1, d_ff), e_map3),
                  pl.BlockSpec((None, 1, d_ff), e_map3),
                  pl.BlockSpec((None, d_ff, D_MODEL), e_map3),
                  pl.BlockSpec((None, 1, D_MODEL), e_map3)],
        out_specs=pl.BlockSpec((tr, D_MODEL), lambda r, te, tv: (r, 0)),
    )
    return pl.pallas_call(
        _expert_kernel,
        grid_spec=grid_spec,
        out_shape=jax.ShapeDtypeStruct((n_rows, D_MODEL), F32),
        compiler_params=_cparams("arbitrary"),
        name="moe_experts",
    )(tile_expert, tile_valid, xs, w1g, w1l, b1g, b1l, w2, b2)


def _combine_kernel(dest_ref, ys_ref, x1_ref, gate_ref, fw_ref, yp_ref, ysm_ref, buf_ref, sem,
                    *, tm, n_prompt_tiles):
    i = pl.program_id(0)

    def start(t, c):
        for j in range(TOP_K):
            pltpu.make_async_copy(ys_ref.at[pl.ds(dest_ref[0, t * TOP_K + j], 1)],
                                  buf_ref.at[j, pl.ds(t, 1)], sem).start()
        return c

    def wait(t, c):
        for j in range(TOP_K):
            pltpu.make_async_copy(ys_ref.at[pl.ds(dest_ref[0, t * TOP_K + j], 1)],
                                  buf_ref.at[j, pl.ds(t, 1)], sem).wait()
        return c

    lax.fori_loop(0, tm, start, 0)
    lax.fori_loop(0, tm, wait, 0)
    gate = gate_ref[...]
    out = x1_ref[...]
    for j in range(TOP_K):
        out = out + gate[:, j:j + 1] * buf_ref[j]
    y = out * lax.rsqrt(jnp.mean(out * out, axis=-1, keepdims=True) + EPS) * fw_ref[...]

    @pl.when(i < n_prompt_tiles)
    def _():
        yp_ref[...] = y

    @pl.when(i >= n_prompt_tiles)
    def _():
        ysm_ref[...] = y


def _combine(dest, ys, x1, gate, final_w, n_p, n_s, tm):
    n = n_p + n_s
    nt = n // tm
    npt = n_p // tm
    row = lambda i: (i, 0)
    return pl.pallas_call(
        functools.partial(_combine_kernel, tm=tm, n_prompt_tiles=npt),
        grid=(nt,),
        in_specs=[pl.BlockSpec((None, 1, tm * TOP_K), lambda i: (i, 0, 0), memory_space=pltpu.SMEM),
                  pl.BlockSpec(memory_space=pl.ANY),
                  pl.BlockSpec((tm, D_MODEL), row),
                  pl.BlockSpec((tm, LANES), row),
                  pl.BlockSpec((1, D_MODEL), lambda i: (0, 0))],
        out_specs=[pl.BlockSpec((tm, D_MODEL), lambda i: (jnp.minimum(i, npt - 1), 0)),
                   pl.BlockSpec((tm, D_MODEL), lambda i: (0, 0))],
        out_shape=[jax.ShapeDtypeStruct((n_p, D_MODEL), F32), jax.ShapeDtypeStruct((n_s, D_MODEL), F32)],
        scratch_shapes=[pltpu.VMEM((TOP_K, tm, D_MODEL), F32), pltpu.SemaphoreType.DMA(())],
        compiler_params=_cparams("arbitrary"),
        name="moe_combine",
    )(dest.reshape(nt, 1, tm * TOP_K), ys, x1, gate, final_w)


def _t5_bucket_np(n):
    n = np.maximum(np.asarray(n, np.int64), 0)
    max_exact = N_BUCKETS // 2
    nf = np.maximum(n, max_exact).astype(np.float32)
    large = max_exact + (np.log(nf / np.float32(max_exact)) / np.float32(math.log(MAX_DISTANCE / max_exact))
                         * np.float32(N_BUCKETS - max_exact)).astype(np.int32)
    return np.where(n < max_exact, n, np.minimum(large, N_BUCKETS - 1)).astype(np.int32)


def _bias_tiles_kernel(rb_ref, idx_ref, o_ref):
    h = pl.program_id(0)
    far = rb_ref[N_BUCKETS - 1, h]
    for d in range(2):
        idx = idx_ref[d]
        acc = jnp.zeros(idx.shape, F32)
        for bkt in range(N_BUCKETS - 1):
            acc = jnp.where(idx == bkt, rb_ref[bkt, h] - far, acc)
        o_ref[d] = jnp.where(idx < 0, NEG, acc)
    o_ref[2] = jnp.zeros(o_ref.shape[1:], F32)


def _prompt_bias_tiles(rel_bias, tile):
    assert tile >= MAX_DISTANCE
    far = N_BUCKETS - 1
    ql = np.arange(tile)[None, :]
    kl = np.arange(tile)[:, None]
    rel0 = ql - kl
    rel1 = tile + ql - kl
    assert (_t5_bucket_np(np.arange(tile + 1, 4 * tile)) == far).all()
    idx = np.stack([np.where(rel0 >= 0, _t5_bucket_np(rel0), -1), _t5_bucket_np(rel1)]).astype(np.int32)
    return pl.pallas_call(
        _bias_tiles_kernel,
        grid=(N_HEADS,),
        in_specs=[pl.BlockSpec(memory_space=pltpu.SMEM),
                  pl.BlockSpec((2, tile, tile), lambda h: (0, 0, 0))],
        out_specs=pl.BlockSpec((None, 3, tile, tile), lambda h: (h, 0, 0, 0)),
        out_shape=jax.ShapeDtypeStruct((N_HEADS, 3, tile, tile), F32),
        compiler_params=_cparams("parallel"),
        name="bias_tiles",
    )(rel_bias.astype(F32), jnp.asarray(idx))


def _sample_bias_table(rel_bias, past):
    far = N_BUCKETS - 1
    assert (_t5_bucket_np(np.arange(PAGE + 1, past + 1)) == far).all()
    shifted = (rel_bias - rel_bias[far:far + 1]).T
    bucket = _t5_bucket_np(PAGE - np.arange(PAGE))
    per_pos = shifted[:, bucket]
    same = np.eye(N_HEADS, dtype=bool)[:, None, :]
    tab = jnp.where(same, per_pos[:, :, None], 0.0).reshape(N_HEADS, PAGE * N_HEADS)
    bias0 = shifted[:, _t5_bucket_np(0)].reshape(N_HEADS, 1)
    return jnp.concatenate([tab, tab], axis=0).astype(F32), bias0.astype(F32)


def kernel(x_prompt, x_sample, cache_k, cache_v, state_hgrn, page_table, norm1_w, w_in, lambda_q1, lambda_k1,
           lambda_q2, lambda_k2, subln_w, lb_logits, hgrn_norm_w, rel_bias, w_pa, w_pb, w_o, norm2_w,
           w_router, b_router, w1, b1, w2, b2, final_norm_w):
    depth = w_in.shape[0]
    assert depth == 1
    l = 0
    b, s, d = x_prompt.shape
    n_seq, t_new, _ = x_sample.shape
    assert d == D_MODEL and t_new == 1
    n_pages = page_table.shape[1]
    past = n_pages * PAGE
    n_p = b * s
    lam_init = 0.8 - 0.6 * math.exp(-0.3 * l)

    lower_bounds = jnp.cumsum(jax.nn.softmax(lb_logits.astype(F32), axis=0), axis=0)
    lb = lower_bounds[l].reshape(1, D_MODEL)
    lamv = jnp.stack([lambda_q1[l], lambda_k1[l], lambda_q2[l], lambda_k2[l]]).astype(F32)
    n1 = norm1_w[l].reshape(1, D_MODEL)
    w_in_bf = w_in[l].astype(BF16)
    sw = subln_w[l].reshape(1, HEAD_W)
    hw = hgrn_norm_w[l].reshape(1, HEAD_W)

    xp = x_prompt.reshape(n_p, D_MODEL)
    qa, ka, va, hq, hk, hg, hv, sog, sga, sgb = _inproj(xp, n1, lb, w_in_bf, 256)
    tile = min(ATTN_TILE, s)
    oa_p = _attn_prompt(lamv, sw, qa, ka, va, _prompt_bias_tiles(rel_bias, tile), b, s, tile, lam_init)
    ob_p, st_p = _hgrn_prompt(hq, hk, hg, hv, sog, hw, b, s, min(HGRN_TIME_BLOCK, s))

    xs = x_sample.reshape(n_seq, D_MODEL)
    qa_s, ka_s, va_s, hq_s, hk_s, hg_s, hv_s, sog_s, sga_s, sgb_s = _inproj(xs, n1, lb, w_in_bf, n_seq)
    heads = lambda a: a.reshape(n_seq, N_HEADS, HEAD_W)
    bias_tab, bias0 = _sample_bias_table(rel_bias, past)
    group = max(g for g in (8, 4, 2, 1) if n_pages % g == 0)
    oa_s = _attn_sample(page_table, lamv, sw, heads(qa_s), heads(ka_s), heads(va_s), bias0, bias_tab,
                        cache_k, cache_v, l, group, lam_init)
    ob_s, st_s = _hgrn_sample(heads(hq_s), heads(hk_s), heads(hg_s), heads(hv_s), heads(sog_s), hw,
                              state_hgrn, l)

    w_r = jnp.pad(w_router[l], ((0, 0), (0, LANES - N_EXPERTS))).astype(BF16)
    b_r = jnp.pad(b_router[l].astype(F32), (0, LANES - N_EXPERTS)).reshape(1, LANES)
    x1, h2, idx, gate, rank, cnt = _postmix(
        (xp, oa_p, ob_p, sga, sgb),
        (xs, oa_s.reshape(n_seq, D_MODEL), ob_s.reshape(n_seq, D_MODEL), sga_s, sgb_s),
        w_pa[l].astype(BF16), w_pb[l].astype(BF16), w_o[l].astype(BF16), norm2_w[l].reshape(1, D_MODEL),
        w_r, b_r, TOK_TILE)

    n = n_p + n_seq
    counts = cnt[0, :N_EXPERTS]
    padded = (counts + ROW_TILE - 1) // ROW_TILE * ROW_TILE
    ends = jnp.cumsum(padded)
    starts = ends - padded
    n_tiles = (n * TOP_K) // ROW_TILE + N_EXPERTS
    tile_row = jnp.arange(n_tiles, dtype=I32) * ROW_TILE
    tile_expert = jnp.minimum(jnp.sum((tile_row[:, None] >= ends[None, :]).astype(I32), axis=1), N_EXPERTS - 1)
    tile_valid = jnp.clip(counts[tile_expert] - (tile_row - starts[tile_expert]), 0, ROW_TILE).astype(I32)
    dest = (starts[idx[:, :TOP_K]] + rank[:, :TOP_K]).astype(I32)

    xs_sorted = _dispatch((starts + counts).astype(I32), ends.astype(I32), (ends[-1:] // ROW_TILE).astype(I32),
                          dest, h2, n_tiles, ROW_TILE, TOK_TILE)
    w1l_ = w1[l]
    w1g = w1l_[..., 0::2].astype(BF16)
    w1l = w1l_[..., 1::2].astype(BF16)
    b1g = b1[l][:, None, 0::2]
    b1l = b1[l][:, None, 1::2]
    ys = _experts(tile_expert, tile_valid, xs_sorted, w1g, w1l, b1g, b1l, w2[l].astype(BF16),
                  b2[l][:, None, :], ROW_TILE)
    y_p, y_s = _combine(dest, ys, x1, gate, final_norm_w.reshape(1, D_MODEL), n_p, n_seq, TOK_TILE)

    kv_p = lambda a: a.reshape(1, b, s, N_HEADS, HEAD_W)
    kv_s = lambda a: a.reshape(1, n_seq, 1, N_HEADS, HEAD_W)
    return (y_p.reshape(b, s, D_MODEL), y_s.reshape(n_seq, 1, D_MODEL), kv_p(ka), kv_p(va),
            st_p[None].astype(state_hgrn.dtype), kv_s(ka_s), kv_s(va_s), st_s)
```

```python
import functools
import math

import numpy as np
import jax
import jax.numpy as jnp
from jax import lax
from jax.experimental import pallas as pl
from jax.experimental.pallas import tpu as pltpu

F32 = jnp.float32
BF16 = jnp.bfloat16
I32 = jnp.int32

D_MODEL = 1024
N_HEADS = 8
HEAD_W = 128
A_HD = 64
A_SCALE = A_HD ** -0.5
PAGE = 128
N_BUCKETS = 32
MAX_DISTANCE = 128
N_EXPERTS = 32
TOP_K = 4
SWIGLU_ALPHA = 1.702
SWIGLU_LIMIT = 7.0
EPS = 1e-5
NEG = -1e30

LANES = 128
VMEM_LIMIT = 56 * 1024 * 1024

ATTN_TILE = 512
TOK_TILE = 128
ROW_TILE = 256
HGRN_CHUNK = 64
HGRN_SUB = 16
HGRN_TIME_BLOCK = 512
DECODE_SLOTS = 3


def _cparams(*sem):
    return pltpu.CompilerParams(dimension_semantics=sem, vmem_limit_bytes=VMEM_LIMIT)


def _sigmoid(x):
    return 1.0 / (1.0 + jnp.exp(-x))


def _dot(a, b):
    return jnp.dot(a, b, preferred_element_type=F32)


def _dot_nt(a, b):
    return lax.dot_general(a, b, (((1,), (1,)), ((), ())), preferred_element_type=F32)


def _dot_tn(a, b):
    return lax.dot_general(a, b, (((0,), (0,)), ((), ())), preferred_element_type=F32)


def _lam(lam_ref, lam_init):
    lv = lam_ref[...]
    e1 = jnp.exp(jnp.sum(lv[0:1] * lv[1:2], axis=-1, keepdims=True))
    e2 = jnp.exp(jnp.sum(lv[2:3] * lv[3:4], axis=-1, keepdims=True))
    return e1 - e2 + lam_init


def _head_rmsnorm(o, w):
    return o * lax.rsqrt(jnp.mean(o * o, axis=-1, keepdims=True) + EPS) * w


def _inproj_kernel(x_ref, nw_ref, lb_ref, w_ref, qa_ref, ka_ref, va_ref, hq_ref, hk_ref, hg_ref,
                   hv_ref, sog_ref, sga_ref, sgb_ref):
    x = x_ref[...]
    h = x * lax.rsqrt(jnp.mean(x * x, axis=-1, keepdims=True) + EPS) * nw_ref[...]
    hb = h.astype(BF16)

    def proj(s):
        return _dot(hb, w_ref[:, s * D_MODEL:(s + 1) * D_MODEL])

    qa_ref[...] = (proj(0) * A_SCALE).astype(BF16)
    ka_ref[...] = proj(1)
    va_ref[...] = proj(2)
    qb = proj(3)
    hq_ref[...] = qb * _sigmoid(qb)
    lb = lb_ref[...]
    f = lb + (1.0 - lb) * _sigmoid(proj(4))
    hk_ref[...] = 1.0 - f
    hg_ref[...] = jnp.log(f)
    hv_ref[...] = proj(5).astype(BF16)
    sog_ref[...] = _sigmoid(proj(6))
    sga_ref[...] = _sigmoid(proj(7))
    sgb_ref[...] = _sigmoid(proj(8))


def _inproj(x, norm_w, lb, w_in_bf, tm):
    n = x.shape[0]
    row = lambda i: (i, 0)
    fixed = lambda i: (0, 0)
    f32_out = jax.ShapeDtypeStruct((n, D_MODEL), F32)
    bf_out = jax.ShapeDtypeStruct((n, D_MODEL), BF16)
    blk = pl.BlockSpec((tm, D_MODEL), row)
    return pl.pallas_call(
        _inproj_kernel,
        grid=(n // tm,),
        in_specs=[blk,
                  pl.BlockSpec((1, D_MODEL), fixed),
                  pl.BlockSpec((1, D_MODEL), fixed),
                  pl.BlockSpec(w_in_bf.shape, fixed, pipeline_mode=pl.Buffered(1))],
        out_specs=[blk] * 10,
        out_shape=[bf_out, f32_out, f32_out, f32_out, f32_out, f32_out, bf_out, f32_out, f32_out, f32_out],
        compiler_params=_cparams("parallel"),
        name="inproj",
    )(x, norm_w, lb, w_in_bf)


def _attn_kernel(lam_ref, sw_ref, q_ref, k_ref, v_ref, bias_ref, o_ref, q2_ref, m_ref, l_ref, acc_ref,
                 *, tile, lam_init):
    qi = pl.program_id(2)
    q = q_ref[...]
    lane = lax.broadcasted_iota(I32, q.shape, 1)
    q2_ref[:tile] = jnp.where(lane < A_HD, q, jnp.zeros_like(q))
    q2_ref[tile:] = jnp.where(lane >= A_HD, q, jnp.zeros_like(q))
    m_ref[...] = jnp.full(m_ref.shape, NEG, F32)
    l_ref[...] = jnp.zeros_like(l_ref)
    acc_ref[...] = jnp.zeros_like(acc_ref)

    def body(kj, carry):
        sl = pl.ds(pl.multiple_of(kj * tile, tile), tile)
        kb = k_ref[sl, :].astype(BF16)
        vb = v_ref[sl, :].astype(BF16)
        bias = bias_ref[jnp.minimum(qi - kj, 2)]
        s = _dot_nt(kb, q2_ref[...])
        s = jnp.concatenate([s[:, :tile] + bias, s[:, tile:] + bias], axis=1)
        m = m_ref[...]
        mn = jnp.maximum(m, jnp.max(s, axis=0, keepdims=True))
        p = jnp.exp(s - mn)
        al = jnp.exp(m - mn)
        m_ref[...] = mn
        l_ref[...] = al * l_ref[...] + jnp.sum(p, axis=0, keepdims=True)
        acc_ref[...] = al * acc_ref[...] + _dot_tn(vb, p.astype(BF16))
        return carry

    lax.fori_loop(0, qi + 1, body, 0)
    lam = _lam(lam_ref, lam_init)
    on = acc_ref[...] / l_ref[...]
    o = (on[:, :tile] - lam * on[:, tile:]).T
    o_ref[...] = (_head_rmsnorm(o, sw_ref[...]) * (1.0 - lam_init)).astype(BF16)


def _attn_prompt(lamv, subln_w, qa, ka, va, bias_tab, b, s, tile, lam_init):
    nq = s // tile
    return pl.pallas_call(
        functools.partial(_attn_kernel, tile=tile, lam_init=lam_init),
        grid=(b, N_HEADS, nq),
        in_specs=[pl.BlockSpec((4, A_HD), lambda bi, h, qi: (0, 0)),
                  pl.BlockSpec((1, HEAD_W), lambda bi, h, qi: (0, 0)),
                  pl.BlockSpec((tile, HEAD_W), lambda bi, h, qi: (bi * nq + qi, h)),
                  pl.BlockSpec((s, HEAD_W), lambda bi, h, qi: (bi, h)),
                  pl.BlockSpec((s, HEAD_W), lambda bi, h, qi: (bi, h)),
                  pl.BlockSpec((None, 3, tile, tile), lambda bi, h, qi: (h, 0, 0, 0))],
        out_specs=pl.BlockSpec((tile, HEAD_W), lambda bi, h, qi: (bi * nq + qi, h)),
        out_shape=jax.ShapeDtypeStruct((b * s, D_MODEL), BF16),
        scratch_shapes=[pltpu.VMEM((2 * tile, HEAD_W), BF16), pltpu.VMEM((1, 2 * tile), F32),
                        pltpu.VMEM((1, 2 * tile), F32), pltpu.VMEM((HEAD_W, 2 * tile), F32)],
        compiler_params=_cparams("parallel", "parallel", "parallel"),
        name="attn_prompt",
    )(lamv, subln_w, qa, ka, va, bias_tab)


def _split3(g):
    g1 = g.astype(BF16)
    r1 = g - g1.astype(F32)
    g2 = r1.astype(BF16)
    g3 = (r1 - g2.astype(F32)).astype(BF16)
    return g1, g2, g3


def _hgrn_kernel(q_ref, k_ref, g_ref, v_ref, og_ref, nw_ref, o_ref, s_ref, st_ref, *, tb):
    c, sub = HGRN_CHUNK, HGRN_SUB
    t = pl.program_id(1)

    @pl.when(t == 0)
    def _():
        st_ref[...] = jnp.zeros_like(st_ref)

    row = lax.broadcasted_iota(I32, (c, c), 0)
    col = lax.broadcasted_iota(I32, (c, c), 1)
    tril = jnp.where(row >= col, 1.0, 0.0).astype(BF16)
    rowk = lax.broadcasted_iota(I32, (c, D_MODEL), 0)
    srow = lax.broadcasted_iota(I32, (sub, c), 0)
    scol = lax.broadcasted_iota(I32, (sub, c), 1)

    def body(ci, carry):
        sl = pl.ds(pl.multiple_of(ci * c, c), c)
        q = q_ref[sl, :]
        k = k_ref[sl, :]
        vb = v_ref[sl, :]
        g1, g2, g3 = _split3(g_ref[sl, :])
        cum = _dot(tril, g1) + _dot(tril, g2) + _dot(tril, g3)
        qe = (q * jnp.exp(cum)).astype(BF16)
        last = cum[c - 1:c, :]
        kd = (k * jnp.exp(last - cum)).astype(BF16)
        el = jnp.exp(last)
        qts, kts = [], []
        for i in range(c // sub):
            lo, hi = i * sub, (i + 1) * sub
            mid = lo + sub // 2
            r = cum[mid:mid + 1, :]
            qts.append((q[lo:hi] * jnp.exp(cum[lo:hi] - r)).astype(BF16))
            kts.append((k * jnp.exp(jnp.where(rowk < hi, r - cum, -jnp.inf))).astype(BF16))
        og = og_ref[sl, :]
        nw = nw_ref[...]
        sts = [st_ref[h] for h in range(N_HEADS)]
        hss = [slice(h * HEAD_W, (h + 1) * HEAD_W) for h in range(N_HEADS)]
        atts = [[_dot_nt(qts[i][:, hs], kts[i][:, hs]) for i in range(c // sub)] for hs in hss]
        inters = [_dot_nt(qe[:, hs], sts[h].astype(BF16)) for h, hs in enumerate(hss)]
        upds = [_dot_tn(vb[:, hs], kd[:, hs]) for hs in hss]
        o_heads = []
        for h, hs in enumerate(hss):
            outs = [_dot(jnp.where(scol <= srow + i * sub, atts[h][i], 0.0).astype(BF16), vb[:, hs])
                    for i in range(c // sub)]
            o = inters[h] + jnp.concatenate(outs, axis=0)
            o_heads.append(_head_rmsnorm(o * og[:, hs], nw).astype(BF16))
        for h, hs in enumerate(hss):
            st_ref[h] = sts[h] * el[:, hs] + upds[h]
        o_ref[sl, :] = jnp.concatenate(o_heads, axis=1)
        return carry

    lax.fori_loop(0, tb // c, body, 0)

    @pl.when(t == pl.num_programs(1) - 1)
    def _():
        for h in range(N_HEADS):
            s_ref[h] = st_ref[h].T


def _hgrn_prompt(hq, hk, hg, hv, sog, norm_w, b, s, tb):
    nt = s // tb
    blk = pl.BlockSpec((tb, D_MODEL), lambda bi, t: (bi * nt + t, 0))
    return pl.pallas_call(
        functools.partial(_hgrn_kernel, tb=tb),
        grid=(b, nt),
        in_specs=[blk, blk, blk, blk, blk, pl.BlockSpec((1, HEAD_W), lambda bi, t: (0, 0))],
        out_specs=[blk, pl.BlockSpec((None, N_HEADS, HEAD_W, HEAD_W), lambda bi, t: (bi, 0, 0, 0))],
        out_shape=[jax.ShapeDtypeStruct((b * s, D_MODEL), BF16),
                   jax.ShapeDtypeStruct((b, N_HEADS, HEAD_W, HEAD_W), F32)],
        scratch_shapes=[pltpu.VMEM((N_HEADS, HEAD_W, HEAD_W), F32)],
        compiler_params=_cparams("parallel", "arbitrary"),
        name="hgrn_prompt",
    )(hq, hk, hg, hv, sog, norm_w)


def _decode_kernel(pt_ref, lam_ref, sw_ref, q_ref, kn_ref, vn_ref, b0_ref, bias_ref, ck_ref, cv_ref, o_ref,
                   kbuf, vbuf, ksem, vsem, m_ref, l_ref, acc_ref, *, layer, group, steps_per_seq, lam_init):
    step = pl.program_id(0)
    n_steps = pl.num_programs(0)
    j = step % steps_per_seq
    slot = step % DECODE_SLOTS

    def page_copies(s, sl):
        n_, j_ = s // steps_per_seq, s % steps_per_seq
        cps = []
        for g in range(group):
            page = pt_ref[n_, j_ * group + g]
            cps.append(pltpu.make_async_copy(ck_ref.at[layer, page], kbuf.at[sl, g], ksem.at[sl]))
            cps.append(pltpu.make_async_copy(cv_ref.at[layer, page], vbuf.at[sl, g], vsem.at[sl]))
        return cps

    for ahead in range(DECODE_SLOTS - 1):
        @pl.when((step == 0) & (ahead < n_steps))
        def _():
            for cp in page_copies(ahead, ahead):
                cp.start()

    nxt = step + DECODE_SLOTS - 1

    @pl.when(nxt < n_steps)
    def _():
        for cp in page_copies(nxt, nxt % DECODE_SLOTS):
            cp.start()

    for cp in page_copies(step, slot):
        cp.wait()

    @pl.when(j == 0)
    def _():
        m_ref[...] = jnp.full(m_ref.shape, NEG, F32)
        l_ref[...] = jnp.zeros_like(l_ref)
        acc_ref[...] = jnp.zeros_like(acc_ref)

    q = q_ref[...].astype(F32)
    lane = lax.broadcasted_iota(I32, q.shape, 1)
    q1 = jnp.where(lane < A_HD, q, 0.0)
    q2 = jnp.where(lane >= A_HD, q, 0.0)
    a = jnp.concatenate([q1, q2], axis=0).astype(BF16)
    m, l, acc = m_ref[...], l_ref[...], acc_ref[...]
    cols = group * PAGE * N_HEADS
    kb = kbuf[slot].reshape(cols, HEAD_W).astype(BF16)
    vb = vbuf[slot].reshape(cols, HEAD_W).astype(BF16)
    s = _dot_nt(a, kb)
    srow = lax.broadcasted_iota(I32, s.shape, 0)
    scol = lax.broadcasted_iota(I32, s.shape, 1)
    s = jnp.where((srow & (N_HEADS - 1)) == (scol & (N_HEADS - 1)), s, NEG)
    near = jnp.where(j == steps_per_seq - 1, 1.0, 0.0)
    s = jnp.concatenate([s[:, :cols - PAGE * N_HEADS], s[:, cols - PAGE * N_HEADS:] + near * bias_ref[...]], axis=1)
    mn = jnp.maximum(m, jnp.max(s, axis=-1, keepdims=True))
    p = jnp.exp(s - mn)
    al = jnp.exp(m - mn)
    l = al * l + jnp.sum(p, axis=-1, keepdims=True)
    acc = al * acc + _dot(p.astype(BF16), vb)
    m = mn
    m_ref[...], l_ref[...], acc_ref[...] = m, l, acc

    @pl.when(j == steps_per_seq - 1)
    def _():
        kn = kn_ref[...]
        vn = vn_ref[...]
        prod = q * kn
        s1 = jnp.sum(jnp.where(lane < A_HD, prod, 0.0), axis=-1, keepdims=True) + b0_ref[...]
        s2 = jnp.sum(jnp.where(lane >= A_HD, prod, 0.0), axis=-1, keepdims=True) + b0_ref[...]
        outs = []
        for c, sn in enumerate((s1, s2)):
            mc = m[c * N_HEADS:(c + 1) * N_HEADS]
            lc = l[c * N_HEADS:(c + 1) * N_HEADS]
            ac = acc[c * N_HEADS:(c + 1) * N_HEADS]
            mn = jnp.maximum(mc, sn)
            pn = jnp.exp(sn - mn)
            al = jnp.exp(mc - mn)
            outs.append((al * ac + pn * vn) / (al * lc + pn))
        lam = _lam(lam_ref, lam_init)
        o = outs[0] - lam * outs[1]
        o_ref[...] = _head_rmsnorm(o, sw_ref[...]) * (1.0 - lam_init)


def _attn_sample(page_table, lamv, subln_w, q_s, k_new, v_new, bias0, bias_last, cache_k, cache_v,
                 layer, group, lam_init):
    n_seq, n_pages = page_table.shape
    sps = n_pages // group
    head_blk = pl.BlockSpec((None, N_HEADS, HEAD_W), lambda st, pt: (st // sps, 0, 0))
    fixed = lambda st, pt: (0, 0)
    page_buf = pltpu.VMEM((DECODE_SLOTS, group, PAGE, N_HEADS, HEAD_W), F32)
    page_sem = pltpu.SemaphoreType.DMA((DECODE_SLOTS,))
    grid_spec = pltpu.PrefetchScalarGridSpec(
        num_scalar_prefetch=1,
        grid=(n_seq * sps,),
        in_specs=[pl.BlockSpec((4, A_HD), fixed),
                  pl.BlockSpec((1, HEAD_W), fixed),
                  head_blk, head_blk, head_blk,
                  pl.BlockSpec((N_HEADS, 1), fixed),
                  pl.BlockSpec((2 * N_HEADS, PAGE * N_HEADS), fixed),
                  pl.BlockSpec(memory_space=pl.ANY),
                  pl.BlockSpec(memory_space=pl.ANY)],
        out_specs=head_blk,
        scratch_shapes=[page_buf, page_buf, page_sem, page_sem,
                        pltpu.VMEM((2 * N_HEADS, 1), F32), pltpu.VMEM((2 * N_HEADS, 1), F32),
                        pltpu.VMEM((2 * N_HEADS, HEAD_W), F32)],
    )
    return pl.pallas_call(
        functools.partial(_decode_kernel, layer=layer, group=group, steps_per_seq=sps, lam_init=lam_init),
        grid_spec=grid_spec,
        out_shape=jax.ShapeDtypeStruct((n_seq, N_HEADS, HEAD_W), F32),
        compiler_params=_cparams("arbitrary"),
        name="attn_sample",
    )(page_table, lamv, subln_w, q_s, k_new, v_new, bias0, bias_last, cache_k, cache_v)


def _hgrn_step_kernel(q_ref, k_ref, g_ref, v_ref, og_ref, nw_ref, s_ref, o_ref, so_ref):
    f = jnp.exp(g_ref[...])
    stack = jnp.concatenate([f, k_ref[...], q_ref[...],
                             jnp.zeros((HEAD_W - 3 * N_HEADS, HEAD_W), F32)], axis=0)
    cols = stack.T
    v = v_ref[...].astype(F32)
    outs = []
    for h in range(N_HEADS):
        fc = cols[:, h:h + 1]
        kc = cols[:, N_HEADS + h:N_HEADS + h + 1]
        qc = cols[:, 2 * N_HEADS + h:2 * N_HEADS + h + 1]
        s_new = fc * s_ref[h].astype(F32) + kc * v[h:h + 1, :]
        so_ref[h] = s_new.astype(so_ref.dtype)
        outs.append(jnp.sum(qc * s_new, axis=0, keepdims=True))
    o = jnp.concatenate(outs, axis=0) * og_ref[...]
    o_ref[...] = _head_rmsnorm(o, nw_ref[...])


def _hgrn_sample(hq, hk, hg, hv, sog, norm_w, state, layer):
    n_seq = state.shape[1]
    head_blk = pl.BlockSpec((None, N_HEADS, HEAD_W), lambda n: (n, 0, 0))
    st_in = pl.BlockSpec((None, None, N_HEADS, HEAD_W, HEAD_W), lambda n: (layer, n, 0, 0, 0))
    st_out = pl.BlockSpec((None, None, N_HEADS, HEAD_W, HEAD_W), lambda n: (0, n, 0, 0, 0))
    return pl.pallas_call(
        _hgrn_step_kernel,
        grid=(n_seq,),
        in_specs=[head_blk] * 5 + [pl.BlockSpec((1, HEAD_W), lambda n: (0, 0)), st_in],
        out_specs=[head_blk, st_out],
        out_shape=[jax.ShapeDtypeStruct((n_seq, N_HEADS, HEAD_W), F32),
                   jax.ShapeDtypeStruct((1,) + state.shape[1:], state.dtype)],
        compiler_params=_cparams("parallel"),
        name="hgrn_sample",
    )(hq, hk, hg, hv, sog, norm_w, state)


def _postmix_kernel(xp_ref, oap_ref, obp_ref, gap_ref, gbp_ref, xs_ref, oas_ref, obs_ref, gas_ref, gbs_ref,
                    wpa_ref, wpb_ref, wo_ref, n2_ref, wr_ref, br_ref,
                    x1_ref, h2_ref, idx_ref, gate_ref, rank_ref, cnt_ref, carry_ref, *, n_prompt_tiles):
    i = pl.program_id(0)
    is_s = i >= n_prompt_tiles

    @pl.when(i == 0)
    def _():
        carry_ref[...] = jnp.zeros_like(carry_ref)

    def pick(p_ref, s_ref):
        return jnp.where(is_s, s_ref[...], p_ref[...])

    x = pick(xp_ref, xs_ref)
    oa = pick(oap_ref, oas_ref).astype(BF16)
    ob = pick(obp_ref, obs_ref).astype(BF16)
    m = pick(gap_ref, gas_ref) * _dot(oa, wpa_ref[...]) + pick(gbp_ref, gbs_ref) * _dot(ob, wpb_ref[...])
    x1 = x + _dot(m.astype(BF16), wo_ref[...])
    x1_ref[...] = x1
    h2 = x1 * lax.rsqrt(jnp.mean(x1 * x1, axis=-1, keepdims=True) + EPS) * n2_ref[...]
    h2_ref[...] = h2
    logits = _dot(h2.astype(BF16), wr_ref[...]) + br_ref[...]
    tm = logits.shape[0]
    lane = lax.broadcasted_iota(I32, logits.shape, 1)
    logits = jnp.where(lane < N_EXPERTS, logits, -jnp.inf)
    vals, idxs = [], []
    for _ in range(TOP_K):
        mx = jnp.max(logits, axis=-1, keepdims=True)
        ix = jnp.min(jnp.where(logits == mx, lane, LANES), axis=-1, keepdims=True)
        vals.append(mx)
        idxs.append(ix)
        logits = jnp.where(lane == ix, -jnp.inf, logits)
    es = [jnp.exp(v - vals[0]) for v in vals]
    den = es[0] + es[1] + es[2] + es[3]
    onehots = [lane == ix for ix in idxs]
    chosen = jnp.where(onehots[0] | onehots[1] | onehots[2] | onehots[3], 1.0, 0.0)
    row = lax.broadcasted_iota(I32, (tm, tm), 0)
    col = lax.broadcasted_iota(I32, (tm, tm), 1)
    strict = jnp.where(row > col, 1.0, 0.0).astype(BF16)
    before = _dot(strict, chosen.astype(BF16)) + carry_ref[...]
    idx_out = jnp.zeros(logits.shape, I32)
    gate_out = jnp.zeros(logits.shape, F32)
    rank_out = jnp.zeros(logits.shape, F32)
    for jj in range(TOP_K):
        rk = jnp.sum(jnp.where(onehots[jj], before, 0.0), axis=-1, keepdims=True)
        idx_out = jnp.where(lane == jj, idxs[jj], idx_out)
        gate_out = jnp.where(lane == jj, es[jj] / den, gate_out)
        rank_out = jnp.where(lane == jj, rk, rank_out)
    idx_ref[...] = idx_out
    gate_ref[...] = gate_out
    rank_ref[...] = rank_out.astype(I32)
    carry = carry_ref[...] + jnp.sum(chosen, axis=0, keepdims=True)
    carry_ref[...] = carry
    cnt_ref[...] = carry.astype(I32)


def _postmix(prompt_in, sample_in, w_pa, w_pb, w_o, norm2_w, w_r, b_r, tm):
    n_p = prompt_in[0].shape[0]
    n_s = sample_in[0].shape[0]
    assert n_s == tm and n_p % tm == 0
    npt = n_p // tm
    n = n_p + n_s
    p_blk = pl.BlockSpec((tm, D_MODEL), lambda i: (jnp.minimum(i, npt - 1), 0))
    s_blk = pl.BlockSpec((tm, D_MODEL), lambda i: (0, 0))
    fixed = lambda i: (0, 0)
    w_blk = pl.BlockSpec((D_MODEL, D_MODEL), fixed)
    vec = pl.BlockSpec((1, D_MODEL), fixed)
    row = lambda i: (i, 0)
    return pl.pallas_call(
        functools.partial(_postmix_kernel, n_prompt_tiles=npt),
        grid=(npt + 1,),
        in_specs=[p_blk] * 5 + [s_blk] * 5 + [w_blk, w_blk, w_blk, vec,
                                              pl.BlockSpec((D_MODEL, LANES), fixed),
                                              pl.BlockSpec((1, LANES), fixed)],
        out_specs=[pl.BlockSpec((tm, D_MODEL), row), pl.BlockSpec((tm, D_MODEL), row),
                   pl.BlockSpec((tm, LANES), row), pl.BlockSpec((tm, LANES), row),
                   pl.BlockSpec((tm, LANES), row), pl.BlockSpec((1, LANES), fixed)],
        out_shape=[jax.ShapeDtypeStruct((n, D_MODEL), F32), jax.ShapeDtypeStruct((n, D_MODEL), F32),
                   jax.ShapeDtypeStruct((n, LANES), I32), jax.ShapeDtypeStruct((n, LANES), F32),
                   jax.ShapeDtypeStruct((n, LANES), I32), jax.ShapeDtypeStruct((1, LANES), I32)],
        scratch_shapes=[pltpu.VMEM((1, LANES), F32)],
        compiler_params=_cparams("arbitrary"),
        name="postmix",
    )(*prompt_in, *sample_in, w_pa, w_pb, w_o, norm2_w, w_r, b_r)


def _row_copy(src, s_row, dst, d_row, sem):
    return pltpu.make_async_copy(src.at[pl.ds(s_row, 1)], dst.at[pl.ds(d_row, 1)], sem)


def _dispatch_kernel(pad_lo_ref, pad_hi_ref, used_ref, dest_ref, h2_ref, xs_ref, zero_ref, sem, zsem,
                     *, tm, tr, n_tiles):
    i = pl.program_id(0)

    @pl.when(i == 0)
    def _():
        zero_ref[...] = jnp.zeros_like(zero_ref)

    def token_copy(t, j):
        return _row_copy(h2_ref, t, xs_ref, dest_ref[0, t * TOP_K + j], sem)

    def pad_copy(r):
        return _row_copy(zero_ref, 0, xs_ref, r, zsem)

    def tail_copy():
        row0 = pl.multiple_of((used_ref[0] + i) * tr, tr)
        return pltpu.make_async_copy(zero_ref, xs_ref.at[pl.ds(row0, tr)], zsem)

    def start(t, c):
        for j in range(TOP_K):
            token_copy(t, j).start()
        return c

    def wait(t, c):
        for j in range(TOP_K):
            token_copy(t, j).wait()
        return c

    has_pad = i < N_EXPERTS
    e = jnp.minimum(i, N_EXPERTS - 1)
    lo = jnp.where(has_pad, pad_lo_ref[e], 0)
    hi = jnp.where(has_pad, pad_hi_ref[e], 0)
    has_tail = used_ref[0] + i < n_tiles

    lax.fori_loop(0, tm, start, 0)
    lax.fori_loop(lo, hi, lambda r, c: (pad_copy(r).start(), c)[1], 0)

    @pl.when(has_tail)
    def _():
        tail_copy().start()

    lax.fori_loop(0, tm, wait, 0)
    lax.fori_loop(lo, hi, lambda r, c: (pad_copy(r).wait(), c)[1], 0)

    @pl.when(has_tail)
    def _():
        tail_copy().wait()


def _dispatch(pad_lo, pad_hi, used_tiles, dest, h2, n_tiles, tr, tm):
    n = h2.shape[0]
    nt = n // tm
    assert nt >= N_EXPERTS
    grid_spec = pltpu.PrefetchScalarGridSpec(
        num_scalar_prefetch=3,
        grid=(nt,),
        in_specs=[pl.BlockSpec((None, 1, tm * TOP_K), lambda i, *_: (i, 0, 0), memory_space=pltpu.SMEM),
                  pl.BlockSpec((tm, D_MODEL), lambda i, *_: (i, 0))],
        out_specs=pl.BlockSpec(memory_space=pl.ANY),
        scratch_shapes=[pltpu.VMEM((tr, D_MODEL), F32), pltpu.SemaphoreType.DMA(()), pltpu.SemaphoreType.DMA(())],
    )
    return pl.pallas_call(
        functools.partial(_dispatch_kernel, tm=tm, tr=tr, n_tiles=n_tiles),
        grid_spec=grid_spec,
        out_shape=jax.ShapeDtypeStruct((n_tiles * tr, D_MODEL), F32),
        compiler_params=_cparams("arbitrary"),
        name="moe_dispatch",
    )(pad_lo, pad_hi, used_tiles, dest.reshape(nt, 1, tm * TOP_K), h2)


def _w1_split_kernel(w_ref, g_ref, l_ref, t_ref):
    d_model, d_ff = g_ref.shape
    for c in range(d_model // LANES):
        rows = slice(c * LANES, (c + 1) * LANES)
        t_ref[c] = w_ref[rows, :].T
        g_ref[rows, :] = t_ref[c, pl.ds(0, d_ff, stride=2), :].T.astype(BF16)
        l_ref[rows, :] = t_ref[c, pl.ds(1, d_ff, stride=2), :].T.astype(BF16)


def _w1_split(w1):
    n_e, d_model, two_ff = w1.shape
    d_ff = two_ff // 2
    out = jax.ShapeDtypeStruct((n_e, d_model, d_ff), BF16)
    out_blk = pl.BlockSpec((None, d_model, d_ff), lambda e: (e, 0, 0))
    return pl.pallas_call(
        _w1_split_kernel,
        grid=(n_e,),
        in_specs=[pl.BlockSpec((None, d_model, two_ff), lambda e: (e, 0, 0))],
        out_specs=[out_blk, out_blk],
        out_shape=[out, out],
        scratch_shapes=[pltpu.VMEM((d_model // LANES, two_ff, LANES), F32)],
        compiler_params=_cparams("parallel"),
        name="w1_split",
    )(w1)


def _expert_kernel(te_ref, tv_ref, xs_ref, w1g_ref, w1l_ref, b1g_ref, b1l_ref, w2_ref, b2_ref, ys_ref):
    r = pl.program_id(0)
    nvalid = tv_ref[r]
    rows = lax.broadcasted_iota(I32, xs_ref.shape, 0)
    x = jnp.where(rows < nvalid, xs_ref[...], 0.0).astype(BF16)
    glu = _dot(x, w1g_ref[...]) + b1g_ref[...]
    lin = _dot(x, w1l_ref[...]) + b1l_ref[...]
    glu = jnp.minimum(glu, SWIGLU_LIMIT)
    lin = jnp.clip(lin, -SWIGLU_LIMIT, SWIGLU_LIMIT)
    act = glu * _sigmoid(SWIGLU_ALPHA * glu) * (lin + 1.0)
    ys_ref[...] = _dot(act.astype(BF16), w2_ref[...]) + b2_ref[...]


def _experts(tile_expert, tile_valid, xs, w1g, w1l, b1g, b1l, w2, b2, tr):
    n_rows = xs.shape[0]
    d_ff = w1g.shape[-1]
    e_map3 = lambda r, te, tv: (te[r], 0, 0)
    grid_spec = pltpu.PrefetchScalarGridSpec(
        num_scalar_prefetch=2,
        grid=(n_rows // tr,),
        in_specs=[pl.BlockSpec((tr, D_MODEL), lambda r, te, tv: (r, 0)),
                  pl.BlockSpec((None, D_MODEL, d_ff), e_map3),
                  pl.BlockSpec((None, D_MODEL, d_ff), e_map3),
                  pl.BlockSpec((None, 1, d_ff), e_map3),
                  pl.BlockSpec((None, 1, d_ff), e_map3),
                  pl.BlockSpec((None, d_ff, D_MODEL), e_map3),
                  pl.BlockSpec((None, 1, D_MODEL), e_map3)],
        out_specs=pl.BlockSpec((tr, D_MODEL), lambda r, te, tv: (r, 0)),
    )
    return pl.pallas_call(
        _expert_kernel,
        grid_spec=grid_spec,
        out_shape=jax.ShapeDtypeStruct((n_rows, D_MODEL), F32),
        compiler_params=_cparams("arbitrary"),
        name="moe_experts",
    )(tile_expert, tile_valid, xs, w1g, w1l, b1g, b1l, w2, b2)


def _combine_kernel(dest_ref, ys_ref, x1_ref, gate_ref, fw_ref, yp_ref, ysm_ref, buf_ref, sem,
                    *, tm, n_prompt_tiles):
    i = pl.program_id(0)

    def start(t, c):
        for j in range(TOP_K):
            pltpu.make_async_copy(ys_ref.at[pl.ds(dest_ref[0, t * TOP_K + j], 1)],
                                  buf_ref.at[j, pl.ds(t, 1)], sem).start()
        return c

    def wait(t, c):
        for j in range(TOP_K):
            pltpu.make_async_copy(ys_ref.at[pl.ds(dest_ref[0, t * TOP_K + j], 1)],
                                  buf_ref.at[j, pl.ds(t, 1)], sem).wait()
        return c

    lax.fori_loop(0, tm, start, 0)
    lax.fori_loop(0, tm, wait, 0)
    gate = gate_ref[...]
    out = x1_ref[...]
    for j in range(TOP_K):
        out = out + gate[:, j:j + 1] * buf_ref[j]
    y = out * lax.rsqrt(jnp.mean(out * out, axis=-1, keepdims=True) + EPS) * fw_ref[...]

    @pl.when(i < n_prompt_tiles)
    def _():
        yp_ref[...] = y

    @pl.when(i >= n_prompt_tiles)
    def _():
        ysm_ref[...] = y


def _combine(dest, ys, x1, gate, final_w, n_p, n_s, tm):
    n = n_p + n_s
    nt = n // tm
    npt = n_p // tm
    row = lambda i: (i, 0)
    return pl.pallas_call(
        functools.partial(_combine_kernel, tm=tm, n_prompt_tiles=npt),
        grid=(nt,),
        in_specs=[pl.BlockSpec((None, 1, tm * TOP_K), lambda i: (i, 0, 0), memory_space=pltpu.SMEM),
                  pl.BlockSpec(memory_space=pl.ANY),
                  pl.BlockSpec((tm, D_MODEL), row),
                  pl.BlockSpec((tm, LANES), row),
                  pl.BlockSpec((1, D_MODEL), lambda i: (0, 0))],
        out_specs=[pl.BlockSpec((tm, D_MODEL), lambda i: (jnp.minimum(i, npt - 1), 0)),
                   pl.BlockSpec((tm, D_MODEL), lambda i: (0, 0))],
        out_shape=[jax.ShapeDtypeStruct((n_p, D_MODEL), F32), jax.ShapeDtypeStruct((n_s, D_MODEL), F32)],
        scratch_shapes=[pltpu.VMEM((TOP_K, tm, D_MODEL), F32), pltpu.SemaphoreType.DMA(())],
        compiler_params=_cparams("arbitrary"),
        name="moe_combine",
    )(dest.reshape(nt, 1, tm * TOP_K), ys, x1, gate, final_w)


def _t5_bucket_np(n):
    n = np.maximum(np.asarray(n, np.int64), 0)
    max_exact = N_BUCKETS // 2
    nf = np.maximum(n, max_exact).astype(np.float32)
    large = max_exact + (np.log(nf / np.float32(max_exact)) / np.float32(math.log(MAX_DISTANCE / max_exact))
                         * np.float32(N_BUCKETS - max_exact)).astype(np.int32)
    return np.where(n < max_exact, n, np.minimum(large, N_BUCKETS - 1)).astype(np.int32)


def _bias_tiles_kernel(rb_ref, idx_ref, o_ref):
    h = pl.program_id(0)
    far = rb_ref[N_BUCKETS - 1, h]
    for d in range(2):
        idx = idx_ref[d]
        acc = jnp.zeros(idx.shape, F32)
        for bkt in range(N_BUCKETS - 1):
            acc = jnp.where(idx == bkt, rb_ref[bkt, h] - far, acc)
        o_ref[d] = jnp.where(idx < 0, NEG, acc)
    o_ref[2] = jnp.zeros(o_ref.shape[1:], F32)


def _prompt_bias_tiles(rel_bias, tile):
    assert tile >= MAX_DISTANCE
    far = N_BUCKETS - 1
    ql = np.arange(tile)[None, :]
    kl = np.arange(tile)[:, None]
    rel0 = ql - kl
    rel1 = tile + ql - kl
    assert (_t5_bucket_np(np.arange(tile + 1, 4 * tile)) == far).all()
    idx = np.stack([np.where(rel0 >= 0, _t5_bucket_np(rel0), -1), _t5_bucket_np(rel1)]).astype(np.int32)
    return pl.pallas_call(
        _bias_tiles_kernel,
        grid=(N_HEADS,),
        in_specs=[pl.BlockSpec(memory_space=pltpu.SMEM),
                  pl.BlockSpec((2, tile, tile), lambda h: (0, 0, 0))],
        out_specs=pl.BlockSpec((None, 3, tile, tile), lambda h: (h, 0, 0, 0)),
        out_shape=jax.ShapeDtypeStruct((N_HEADS, 3, tile, tile), F32),
        compiler_params=_cparams("parallel"),
        name="bias_tiles",
    )(rel_bias.astype(F32), jnp.asarray(idx))


def _sample_bias_table(rel_bias, past):
    far = N_BUCKETS - 1
    assert (_t5_bucket_np(np.arange(PAGE + 1, past + 1)) == far).all()
    shifted = (rel_bias - rel_bias[far:far + 1]).T
    bucket = _t5_bucket_np(PAGE - np.arange(PAGE))
    per_pos = shifted[:, bucket]
    same = np.eye(N_HEADS, dtype=bool)[:, None, :]
    tab = jnp.where(same, per_pos[:, :, None], 0.0).reshape(N_HEADS, PAGE * N_HEADS)
    bias0 = shifted[:, _t5_bucket_np(0)].reshape(N_HEADS, 1)
    return jnp.concatenate([tab, tab], axis=0).astype(F32), bias0.astype(F32)


def kernel(x_prompt, x_sample, cache_k, cache_v, state_hgrn, page_table, norm1_w, w_in, lambda_q1, lambda_k1,
           lambda_q2, lambda_k2, subln_w, lb_logits, hgrn_norm_w, rel_bias, w_pa, w_pb, w_o, norm2_w,
           w_router, b_router, w1, b1, w2, b2, final_norm_w):
    depth = w_in.shape[0]
    assert depth == 1
    l = 0
    b, s, d = x_prompt.shape
    n_seq, t_new, _ = x_sample.shape
    assert d == D_MODEL and t_new == 1
    n_pages = page_table.shape[1]
    past = n_pages * PAGE
    n_p = b * s
    lam_init = 0.8 - 0.6 * math.exp(-0.3 * l)

    lower_bounds = jnp.cumsum(jax.nn.softmax(lb_logits.astype(F32), axis=0), axis=0)
    lb = lower_bounds[l].reshape(1, D_MODEL)
    lamv = jnp.stack([lambda_q1[l], lambda_k1[l], lambda_q2[l], lambda_k2[l]]).astype(F32)
    n1 = norm1_w[l].reshape(1, D_MODEL)
    w_in_bf = w_in[l].astype(BF16)
    sw = subln_w[l].reshape(1, HEAD_W)
    hw = hgrn_norm_w[l].reshape(1, HEAD_W)

    xp = x_prompt.reshape(n_p, D_MODEL)
    qa, ka, va, hq, hk, hg, hv, sog, sga, sgb = _inproj(xp, n1, lb, w_in_bf, 256)
    tile = min(ATTN_TILE, s)
    oa_p = _attn_prompt(lamv, sw, qa, ka, va, _prompt_bias_tiles(rel_bias, tile), b, s, tile, lam_init)
    ob_p, st_p = _hgrn_prompt(hq, hk, hg, hv, sog, hw, b, s, min(HGRN_TIME_BLOCK, s))

    xs = x_sample.reshape(n_seq, D_MODEL)
    qa_s, ka_s, va_s, hq_s, hk_s, hg_s, hv_s, sog_s, sga_s, sgb_s = _inproj(xs, n1, lb, w_in_bf, n_seq)
    heads = lambda a: a.reshape(n_seq, N_HEADS, HEAD_W)
    bias_tab, bias0 = _sample_bias_table(rel_bias, past)
    group = max(g for g in (8, 4, 2, 1) if n_pages % g == 0)
    oa_s = _attn_sample(page_table, lamv, sw, heads(qa_s), heads(ka_s), heads(va_s), bias0, bias_tab,
                        cache_k, cache_v, l, group, lam_init)
    ob_s, st_s = _hgrn_sample(heads(hq_s), heads(hk_s), heads(hg_s), heads(hv_s), heads(sog_s), hw,
                              state_hgrn, l)

    w_r = jnp.pad(w_router[l], ((0, 0), (0, LANES - N_EXPERTS))).astype(BF16)
    b_r = jnp.pad(b_router[l].astype(F32), (0, LANES - N_EXPERTS)).reshape(1, LANES)
    x1, h2, idx, gate, rank, cnt = _postmix(
        (xp, oa_p, ob_p, sga, sgb),
        (xs, oa_s.reshape(n_seq, D_MODEL), ob_s.reshape(n_seq, D_MODEL), sga_s, sgb_s),
        w_pa[l].astype(BF16), w_pb[l].astype(BF16), w_o[l].astype(BF16), norm2_w[l].reshape(1, D_MODEL),
        w_r, b_r, TOK_TILE)

    n = n_p + n_seq
    counts = cnt[0, :N_EXPERTS]
    padded = (counts + ROW_TILE - 1) // ROW_TILE * ROW_TILE
    ends = jnp.cumsum(padded)
    starts = ends - padded
    n_tiles = (n * TOP_K) // ROW_TILE + N_EXPERTS
    tile_row = jnp.arange(n_tiles, dtype=I32) * ROW_TILE
    tile_expert = jnp.minimum(jnp.sum((tile_row[:, None] >= ends[None, :]).astype(I32), axis=1), N_EXPERTS - 1)
    tile_valid = jnp.clip(counts[tile_expert] - (tile_row - starts[tile_expert]), 0, ROW_TILE).astype(I32)
    dest = (starts[idx[:, :TOP_K]] + rank[:, :TOP_K]).astype(I32)

    xs_sorted = _dispatch((starts + counts).astype(I32), ends.astype(I32), (ends[-1:] // ROW_TILE).astype(I32),
                          dest, h2, n_tiles, ROW_TILE, TOK_TILE)
    w1g, w1l = _w1_split(w1[l])
    b1_pairs = b1[l].reshape(N_EXPERTS, 1, -1, 2)
    b1g = b1_pairs[..., 0]
    b1l = b1_pairs[..., 1]
    ys = _experts(tile_expert, tile_valid, xs_sorted, w1g, w1l, b1g, b1l, w2[l].astype(BF16),
                  b2[l][:, None, :], ROW_TILE)
    y_p, y_s = _combine(dest, ys, x1, gate, final_norm_w.reshape(1, D_MODEL), n_p, n_seq, TOK_TILE)

    kv_p = lambda a: a.reshape(1, b, s, N_HEADS, HEAD_W)
    kv_s = lambda a: a.reshape(1, n_seq, 1, N_HEADS, HEAD_W)
    return (y_p.reshape(b, s, D_MODEL), y_s.reshape(n_seq, 1, D_MODEL), kv_p(ka), kv_p(va),
            st_p[None].astype(state_hgrn.dtype), kv_s(ka_s), kv_s(va_s), st_s)
```
